```python
import math
import jax, jax.numpy as jnp
from jax import lax
import numpy as np

D_MODEL = 1024
BATCH = 4
SEQ = 4096
DEPTH = 4
DEC_BATCH = 128
DEC_SEQ = 1
PAST_LEN = 2048
PAGE_SIZE = 128

N_MIXERS = 3
N_A = len(range(0, DEPTH, N_MIXERS))
N_B = len(range(1, DEPTH, N_MIXERS))
N_C = len(range(2, DEPTH, N_MIXERS))

SB_HEADS = 16
SB_HEAD_DIM = D_MODEL // SB_HEADS
DIFF_HEAD_DIM = 64
DIFF_HEADS = D_MODEL // (2 * DIFF_HEAD_DIM)
SSM_D_INNER = 2 * D_MODEL
SSM_HEAD_DIM = 64
SSM_HEADS = SSM_D_INNER // SSM_HEAD_DIM
SSM_GROUPS = 8
SSM_STATE = 128
SSM_CONV = 4
SSM_CONV_DIM = SSM_D_INNER + 2 * SSM_GROUPS * SSM_STATE
SSM_IN_DIM = SSM_D_INNER + SSM_CONV_DIM + SSM_HEADS
SSM_CHUNK = 128
D_FF = 2816
Q_BLOCK = 128
NORM_EPS = 1e-6

kernel_name = 'hybrid_sb_diff_ssd_macaron_step'


def rmsnorm(x, g):
    xf = x.astype(jnp.float32)
    y = xf * lax.rsqrt(jnp.mean(xf * xf, axis=-1, keepdims=True) + NORM_EPS)
    return (y * g.astype(jnp.float32)).astype(x.dtype)


def swiglu(h, w_gate, w_up, w_down):
    return (jax.nn.silu(h @ w_gate) * (h @ w_up)) @ w_down


def gather_pages(pool, page_table):
    g = pool[page_table]
    return g.reshape((g.shape[0], g.shape[1] * g.shape[2]) + g.shape[3:])


def sweep_query_blocks(core, q, q_pos):
    b, tq = q.shape[0], q.shape[1]
    if tq <= Q_BLOCK or tq % Q_BLOCK:
        return core(q, q_pos)
    nb = tq // Q_BLOCK
    qb = jnp.moveaxis(q.reshape((b, nb, Q_BLOCK) + q.shape[2:]), 1, 0)
    pb = q_pos.reshape(nb, Q_BLOCK)
    ob = lax.map(lambda a: core(a[0], a[1]), (qb, pb))
    ob = jnp.moveaxis(ob, 0, 1)
    return ob.reshape((b, tq) + ob.shape[3:])


def stick_breaking_core(q, k, v, q_pos, k_pos):
    z = jnp.einsum('bqhd,bkhd->bhqk', q, k).astype(jnp.float32) * (q.shape[-1] ** -0.5)
    causal = k_pos[None, :] < q_pos[:, None]
    log_1m_beta = jnp.where(causal, jax.nn.log_sigmoid(-z), 0.0)
    tail = lax.cumsum(log_1m_beta, axis=3, reverse=True) - log_1m_beta
    w = jnp.where(causal, jnp.exp(jax.nn.log_sigmoid(z) + tail), 0.0)
    return jnp.einsum('bhqk,bkhd->bqhd', w.astype(v.dtype), v)


def stick_breaking_mixer(h, k_past, v_past, w_qkv, w_o):
    b, t, _ = h.shape
    qkv = (h @ w_qkv).reshape(b, t, 3, SB_HEADS, SB_HEAD_DIM)
    q, k, v = qkv[:, :, 0], qkv[:, :, 1], qkv[:, :, 2]
    k_all = jnp.concatenate([k_past, k], axis=1)
    v_all = jnp.concatenate([v_past, v], axis=1)
    past = k_past.shape[1]
    q_pos = past + jnp.arange(t, dtype=jnp.int32)
    k_pos = jnp.arange(past + t, dtype=jnp.int32)
    o = sweep_query_blocks(lambda qb, pb: stick_breaking_core(qb, k_all, v_all, pb, k_pos), q, q_pos)
    return o.reshape(b, t, D_MODEL) @ w_o, k, v


def alibi_slopes(n_heads):
    return jnp.exp2(-8.0 * jnp.arange(1, n_heads + 1, dtype=jnp.float32) / n_heads)


def diff_core(q, k, v, q_pos, k_pos, lam):
    s = jnp.einsum('bqchd,bkchd->bchqk', q, k).astype(jnp.float32) * (q.shape[-1] ** -0.5)
    dist = (q_pos[:, None] - k_pos[None, :]).astype(jnp.float32)
    s = s - alibi_slopes(q.shape[3])[:, None, None] * dist
    s = jnp.where(dist >= 0, s, -jnp.inf)
    p = jax.nn.softmax(s, axis=-1)
    a = p[:, 0] - lam * p[:, 1]
    return jnp.einsum('bhqk,bkhe->bqhe', a.astype(v.dtype), v)


def diff_mixer(h, k_past, v_past, w_qkv, q_g, k_g, lq1, lk1, lq2, lk2, subln_g, w_o, lam_init):
    b, t, _ = h.shape
    qkv = h @ w_qkv
    q = rmsnorm(qkv[..., :D_MODEL].reshape(b, t, 2, DIFF_HEADS, DIFF_HEAD_DIM), q_g)
    k = rmsnorm(qkv[..., D_MODEL:2 * D_MODEL].reshape(b, t, 2, DIFF_HEADS, DIFF_HEAD_DIM), k_g)
    v = qkv[..., 2 * D_MODEL:].reshape(b, t, DIFF_HEADS, 2 * DIFF_HEAD_DIM)
    f32 = jnp.float32
    lam = (jnp.exp(jnp.sum(lq1.astype(f32) * lk1.astype(f32)))
           - jnp.exp(jnp.sum(lq2.astype(f32) * lk2.astype(f32))) + lam_init)
    k_all = jnp.concatenate([k_past, k], axis=1)
    v_all = jnp.concatenate([v_past, v], axis=1)
    past = k_past.shape[1]
    q_pos = past + jnp.arange(t, dtype=jnp.int32)
    k_pos = jnp.arange(past + t, dtype=jnp.int32)
    o = sweep_query_blocks(lambda qb, pb: diff_core(qb, k_all, v_all, pb, k_pos, lam), q, q_pos)
    o = rmsnorm(o, subln_g) * (1.0 - lam_init)
    return o.reshape(b, t, D_MODEL) @ w_o, k, v


def causal_conv(xbc, conv_state, w, bias):
    t = xbc.shape[1]
    xpad = jnp.concatenate([conv_state.astype(xbc.dtype), xbc], axis=1)
    out = bias
    for j in range(SSM_CONV):
        out = out + xpad[:, j:j + t] * w[j]
    return jax.nn.silu(out), xpad[:, t:]


def ssd_scan(x, dt, a_head, bm, cm, h0):
    b, t, nh, hp = x.shape
    g, n = bm.shape[2], bm.shape[3]
    e = nh // g
    l = SSM_CHUNK if t >= SSM_CHUNK else t
    pad = (-t) % l
    f32 = jnp.float32
    xd = x.astype(f32) * dt[..., None]
    a = dt * a_head
    bf, cf = bm.astype(f32), cm.astype(f32)
    if pad:
        padt = lambda z: jnp.pad(z, [(0, 0), (0, pad)] + [(0, 0)] * (z.ndim - 2))
        xd, a, bf, cf = padt(xd), padt(a), padt(bf), padt(cf)
    c = (t + pad) // l
    xd = xd.reshape(b, c, l, g, e, hp)
    a = a.reshape(b, c, l, g, e)
    bf = bf.reshape(b, c, l, g, n)
    cf = cf.reshape(b, c, l, g, n)
    a_cs = jnp.cumsum(a, axis=2)
    mask = jnp.tril(jnp.ones((l, l), bool))[None, None, :, :, None, None]
    seg = a_cs[:, :, :, None] - a_cs[:, :, None, :]
    decay = jnp.exp(jnp.where(mask, seg, -jnp.inf))
    cb = jnp.einsum('bclgn,bcsgn->bclsg', cf, bf)
    y_diag = jnp.einsum('bclsge,bcsgep->bclgep', cb[..., None] * decay, xd)
    decay_to_end = jnp.exp(a_cs[:, :, -1:] - a_cs)
    chunk_states = jnp.einsum('bclgn,bclgep->bcgepn', bf, xd * decay_to_end[..., None])
    chunk_decay = jnp.exp(a_cs[:, :, -1])

    def step(hc, inp):
        s_c, d_c = inp
        return hc * d_c[..., None, None] + s_c, hc

    h_last, h_in = lax.scan(step, h0.astype(f32).reshape(b, g, e, hp, n),
                            (jnp.moveaxis(chunk_states, 1, 0), jnp.moveaxis(chunk_decay, 1, 0)))
    h_in = jnp.moveaxis(h_in, 0, 1)
    y_off = jnp.einsum('bclgn,bcgepn->bclgep', cf, h_in) * jnp.exp(a_cs)[..., None]
    y = (y_diag + y_off).reshape(b, c * l, nh, hp)[:, :t]
    return y, h_last.reshape(b, nh, hp, n)


def ssm_mixer(h, conv_state, ssm_state, w_in, conv_w, conv_b, dt_bias, a_log, d_skip, norm_g, w_out):
    b, t, _ = h.shape
    zxbcdt = h @ w_in
    z = zxbcdt[..., :SSM_D_INNER]
    xbc = zxbcdt[..., SSM_D_INNER:SSM_D_INNER + SSM_CONV_DIM]
    dt_raw = zxbcdt[..., SSM_D_INNER + SSM_CONV_DIM:]
    xbc, new_conv = causal_conv(xbc, conv_state, conv_w, conv_b)
    gn = SSM_GROUPS * SSM_STATE
    xs = xbc[..., :SSM_D_INNER].reshape(b, t, SSM_HEADS, SSM_HEAD_DIM)
    bm = xbc[..., SSM_D_INNER:SSM_D_INNER + gn].reshape(b, t, SSM_GROUPS, SSM_STATE)
    cm = xbc[..., SSM_D_INNER + gn:].reshape(b, t, SSM_GROUPS, SSM_STATE)
    dt = jax.nn.softplus(dt_raw.astype(jnp.float32) + dt_bias.astype(jnp.float32))
    a_head = -jnp.exp(a_log.astype(jnp.float32))
    y, h_last = ssd_scan(xs, dt, a_head, bm, cm, ssm_state)
    y = y + xs.astype(jnp.float32) * d_skip.astype(jnp.float32)[:, None]
    y = y.reshape(b, t, SSM_D_INNER) * jax.nn.silu(z.astype(jnp.float32))
    y = rmsnorm(y.reshape(b, t, SSM_GROUPS, SSM_D_INNER // SSM_GROUPS),
                norm_g.reshape(SSM_GROUPS, SSM_D_INNER // SSM_GROUPS)).reshape(b, t, SSM_D_INNER)
    return y.astype(h.dtype) @ w_out, new_conv.astype(conv_state.dtype), h_last.astype(ssm_state.dtype)


def setup_inputs(seed: int = 0) -> dict:
    key = jax.random.key(seed)
    ks = jax.random.split(key, 48)
    cnt = [0]

    def nk():
        cnt[0] += 1
        return ks[cnt[0] - 1]

    f32 = jnp.float32
    n_pages = PAST_LEN // PAGE_SIZE
    n_used = DEC_BATCH * n_pages
    n_pool = n_used + (n_used + 3) // 4

    def nrm(shape, scale=1.0):
        return jax.random.normal(nk(), shape, f32) * scale

    def gain(shape):
        return 1.0 + 0.01 * jax.random.normal(nk(), shape, f32)

    def unif(shape, lo, hi):
        return jax.random.uniform(nk(), shape, f32, lo, hi)

    inp = {}
    inp['x_prompt'] = nrm((BATCH, SEQ, D_MODEL))
    inp['x_sample'] = nrm((DEC_BATCH, DEC_SEQ, D_MODEL))
    inp['cache_sb_k'] = nrm((N_A, n_pool, PAGE_SIZE, SB_HEADS, SB_HEAD_DIM))
    inp['cache_sb_v'] = nrm((N_A, n_pool, PAGE_SIZE, SB_HEADS, SB_HEAD_DIM))
    inp['cache_diff_k'] = nrm((N_B, n_pool, PAGE_SIZE, 2, DIFF_HEADS, DIFF_HEAD_DIM))
    inp['cache_diff_v'] = nrm((N_B, n_pool, PAGE_SIZE, DIFF_HEADS, 2 * DIFF_HEAD_DIM))
    inp['state_ssm_conv'] = nrm((N_C, DEC_BATCH, SSM_CONV - 1, SSM_CONV_DIM))
    inp['state_ssm'] = nrm((N_C, DEC_BATCH, SSM_HEADS, SSM_HEAD_DIM, SSM_STATE), 0.1)
    inp['page_table'] = jax.random.permutation(nk(), n_pool)[:n_used].reshape(DEC_BATCH, n_pages).astype(jnp.int32)
    inp['ffn_norm'] = gain((DEPTH, 2, D_MODEL))
    inp['ffn_w_gate'] = nrm((DEPTH, 2, D_MODEL, D_FF), D_MODEL ** -0.5)
    inp['ffn_w_up'] = nrm((DEPTH, 2, D_MODEL, D_FF), D_MODEL ** -0.5)
    inp['ffn_w_down'] = nrm((DEPTH, 2, D_FF, D_MODEL), D_FF ** -0.5)
    inp['mix_norm'] = gain((DEPTH, D_MODEL))
    inp['sb_w_qkv'] = nrm((N_A, D_MODEL, 3 * D_MODEL), D_MODEL ** -0.5)
    inp['sb_w_o'] = nrm((N_A, D_MODEL, D_MODEL), D_MODEL ** -0.5)
    inp['diff_w_qkv'] = nrm((N_B, D_MODEL, 3 * D_MODEL), D_MODEL ** -0.5)
    inp['diff_q_norm'] = gain((N_B, DIFF_HEAD_DIM))
    inp['diff_k_norm'] = gain((N_B, DIFF_HEAD_DIM))
    inp['diff_lambda_q1'] = nrm((N_B, DIFF_HEAD_DIM), 0.1)
    inp['diff_lambda_k1'] = nrm((N_B, DIFF_HEAD_DIM), 0.1)
    inp['diff_lambda_q2'] = nrm((N_B, DIFF_HEAD_DIM), 0.1)
    inp['diff_lambda_k2'] = nrm((N_B, DIFF_HEAD_DIM), 0.1)
    inp['diff_subln'] = gain((N_B, 2 * DIFF_HEAD_DIM))
    inp['diff_w_o'] = nrm((N_B, D_MODEL, D_MODEL), D_MODEL ** -0.5)
    inp['ssm_w_in'] = nrm((N_C, D_MODEL, SSM_IN_DIM), D_MODEL ** -0.5)
    inp['ssm_conv_w'] = nrm((N_C, SSM_CONV, SSM_CONV_DIM), SSM_CONV ** -0.5)
    inp['ssm_conv_b'] = nrm((N_C, SSM_CONV_DIM), 0.01)
    dt0 = jnp.exp(unif((N_C, SSM_HEADS), math.log(1e-3), math.log(1e-1)))
    inp['ssm_dt_bias'] = dt0 + jnp.log(-jnp.expm1(-dt0))
    inp['ssm_a_log'] = jnp.log(unif((N_C, SSM_HEADS), 1.0, 16.0))
    inp['ssm_d'] = gain((N_C, SSM_HEADS))
    inp['ssm_norm'] = gain((N_C, SSM_D_INNER))
    inp['ssm_w_out'] = nrm((N_C, SSM_D_INNER, D_MODEL), SSM_D_INNER ** -0.5)
    return inp


def reference(x_prompt, x_sample, cache_sb_k, cache_sb_v, cache_diff_k, cache_diff_v, state_ssm_conv, state_ssm,
              page_table, ffn_norm, ffn_w_gate, ffn_w_up, ffn_w_down, mix_norm, sb_w_qkv, sb_w_o,
              diff_w_qkv, diff_q_norm, diff_k_norm, diff_lambda_q1, diff_lambda_k1, diff_lambda_q2, diff_lambda_k2,
              diff_subln, diff_w_o, ssm_w_in, ssm_conv_w, ssm_conv_b, ssm_dt_bias, ssm_a_log, ssm_d, ssm_norm,
              ssm_w_out):
    bp = x_prompt.shape[0]

    def half_ffn(x, i, s):
        h = rmsnorm(x, ffn_norm[i, s])
        return x + 0.5 * swiglu(h, ffn_w_gate[i, s], ffn_w_up[i, s], ffn_w_down[i, s])

    xp, xs = x_prompt, x_sample
    sb_kp, sb_vp, sb_ks, sb_vs = [], [], [], []
    d_kp, d_vp, d_ks, d_vs = [], [], [], []
    cv_p, ss_p, cv_s, ss_s = [], [], [], []
    for i in range(DEPTH):
        kind, j = i % N_MIXERS, i // N_MIXERS
        xp, xs = half_ffn(xp, i, 0), half_ffn(xs, i, 0)
        hp, hs = rmsnorm(xp, mix_norm[i]), rmsnorm(xs, mix_norm[i])
        if kind == 0:
            empty = jnp.zeros((bp, 0, SB_HEADS, SB_HEAD_DIM), hp.dtype)
            op, k_new, v_new = stick_breaking_mixer(hp, empty, empty, sb_w_qkv[j], sb_w_o[j])
            sb_kp.append(k_new)
            sb_vp.append(v_new)
            os_, k_new, v_new = stick_breaking_mixer(hs, gather_pages(cache_sb_k[j], page_table),
                                                     gather_pages(cache_sb_v[j], page_table), sb_w_qkv[j], sb_w_o[j])
            sb_ks.append(k_new)
            sb_vs.append(v_new)
        elif kind == 1:
            lam_init = 0.8 - 0.6 * math.exp(-0.3 * i)
            dparams = (diff_w_qkv[j], diff_q_norm[j], diff_k_norm[j], diff_lambda_q1[j], diff_lambda_k1[j],
                       diff_lambda_q2[j], diff_lambda_k2[j], diff_subln[j], diff_w_o[j], lam_init)
            ek = jnp.zeros((bp, 0, 2, DIFF_HEADS, DIFF_HEAD_DIM), hp.dtype)
            ev = jnp.zeros((bp, 0, DIFF_HEADS, 2 * DIFF_HEAD_DIM), hp.dtype)
            op, k_new, v_new = diff_mixer(hp, ek, ev, *dparams)
            d_kp.append(k_new)
            d_vp.append(v_new)
            os_, k_new, v_new = diff_mixer(hs, gather_pages(cache_diff_k[j], page_table),
                                           gather_pages(cache_diff_v[j], page_table), *dparams)
            d_ks.append(k_new)
            d_vs.append(v_new)
        else:
            sparams = (ssm_w_in[j], ssm_conv_w[j], ssm_conv_b[j], ssm_dt_bias[j], ssm_a_log[j], ssm_d[j],
                       ssm_norm[j], ssm_w_out[j])
            conv0 = jnp.zeros((bp, SSM_CONV - 1, SSM_CONV_DIM), hp.dtype)
            ssm0 = jnp.zeros((bp, SSM_HEADS, SSM_HEAD_DIM, SSM_STATE), state_ssm.dtype)
            op, c_new, s_new = ssm_mixer(hp, conv0, ssm0, *sparams)
            cv_p.append(c_new)
            ss_p.append(s_new)
            os_, c_new, s_new = ssm_mixer(hs, state_ssm_conv[j], state_ssm[j], *sparams)
            cv_s.append(c_new)
            ss_s.append(s_new)
        xp, xs = xp + op, xs + os_
        xp, xs = half_ffn(xp, i, 1), half_ffn(xs, i, 1)
    return (xp, xs, jnp.stack(sb_kp), jnp.stack(sb_vp), jnp.stack(sb_ks), jnp.stack(sb_vs),
            jnp.stack(d_kp), jnp.stack(d_vp), jnp.stack(d_ks), jnp.stack(d_vs),
            jnp.stack(cv_p), jnp.stack(ss_p), jnp.stack(cv_s), jnp.stack(ss_s))
```

```python
import functools
import math

import jax
import jax.numpy as jnp
from jax import lax
from jax.experimental import pallas as pl
from jax.experimental.pallas import tpu as pltpu

F32 = jnp.float32
BF16 = jnp.bfloat16

NORM_EPS = 1e-6
N_MIXERS = 3
SB_HEADS = 16
SB_HEAD_DIM = 64
DIFF_HEADS = 8
DIFF_HEAD_DIM = 64
SSM_HEAD_DIM = 64
SSM_HEADS = 32
SSM_GROUPS = 8
SSM_STATE = 128
SSM_CONV = 4
SSM_CHUNK = 128
LANES = 128
VMEM_LIMIT = 48 * 1024 * 1024


def _params(*sem):
    return pltpu.CompilerParams(dimension_semantics=sem, vmem_limit_bytes=VMEM_LIMIT)


def _tile(n, pref):
    t = min(n, pref)
    while n % t:
        t //= 2
    return t


def _dot(a, b):
    return jnp.dot(a, b, preferred_element_type=F32)


def _dot_nt(a, b):
    return lax.dot_general(a, b, (((1,), (1,)), ((), ())), preferred_element_type=F32)


def _dot_tn(a, b):
    return lax.dot_general(a, b, (((0,), (0,)), ((), ())), preferred_element_type=F32)


def _split_dot(x, m):
    hi = x.astype(BF16)
    lo = (x - hi.astype(F32)).astype(BF16)
    return _dot(hi, m) + _dot(lo, m)


def _split_dot_left(m, x):
    hi = x.astype(BF16)
    lo = (x - hi.astype(F32)).astype(BF16)
    return _dot(m, hi) + _dot(m, lo)


def _rms_rows(x, g):
    return x * lax.rsqrt(jnp.mean(x * x, axis=-1, keepdims=True) + NORM_EPS) * g


def _silu(x):
    return x / (1.0 + jnp.exp(-x))


def _softplus(z):
    return jnp.maximum(z, 0.0) + jnp.log(1.0 + jnp.exp(-jnp.abs(z)))


def _ffn_kernel(x_ref, g_ref, wg_ref, wu_ref, wd_ref, o_ref, h_scr, acc_scr):
    f = pl.program_id(1)

    @pl.when(f == 0)
    def _():
        h_scr[...] = _rms_rows(x_ref[...], g_ref[...]).astype(BF16)
        acc_scr[...] = jnp.zeros_like(acc_scr)

    h = h_scr[...]
    gate = _dot(h, wg_ref[...])
    up = _dot(h, wu_ref[...])
    act = (_silu(gate) * up).astype(BF16)
    acc_scr[...] += _dot(act, wd_ref[...])

    @pl.when(f == pl.num_programs(1) - 1)
    def _():
        o_ref[...] = x_ref[...] + 0.5 * acc_scr[...]


def _ffn(x, g, wg, wu, wd):
    m, d = x.shape
    dff = wg.shape[1]
    tm = _tile(m, 1024)
    tf = _tile(dff, 256)
    return pl.pallas_call(
        _ffn_kernel,
        grid=(m // tm, dff // tf),
        in_specs=[
            pl.BlockSpec((tm, d), lambda i, f: (i, 0)),
            pl.BlockSpec((1, d), lambda i, f: (0, 0)),
            pl.BlockSpec((d, tf), lambda i, f: (0, f)),
            pl.BlockSpec((d, tf), lambda i, f: (0, f)),
            pl.BlockSpec((tf, d), lambda i, f: (f, 0)),
        ],
        out_specs=pl.BlockSpec((tm, d), lambda i, f: (i, 0)),
        out_shape=jax.ShapeDtypeStruct((m, d), F32),
        scratch_shapes=[pltpu.VMEM((tm, d), BF16), pltpu.VMEM((tm, d), F32)],
        compiler_params=_params("parallel", "arbitrary"),
        name="ffn",
    )(x, g.reshape(1, d), wg, wu, wd)


def _norm_matmul_kernel(x_ref, g_ref, w_ref, o_ref, h_scr):
    @pl.when(pl.program_id(1) == 0)
    def _():
        h_scr[...] = _rms_rows(x_ref[...], g_ref[...]).astype(BF16)

    o_ref[...] = _dot(h_scr[...], w_ref[...])


def _norm_matmul(x, g, w, tn):
    m, d = x.shape
    n = w.shape[1]
    tm = _tile(m, 512)
    return pl.pallas_call(
        _norm_matmul_kernel,
        grid=(m // tm, n // tn),
        in_specs=[
            pl.BlockSpec((tm, d), lambda i, j: (i, 0)),
            pl.BlockSpec((1, d), lambda i, j: (0, 0)),
            pl.BlockSpec((d, tn), lambda i, j: (0, j)),
        ],
        out_specs=pl.BlockSpec((tm, tn), lambda i, j: (i, j)),
        out_shape=jax.ShapeDtypeStruct((m, n), F32),
        scratch_shapes=[pltpu.VMEM((tm, d), BF16)],
        compiler_params=_params("parallel", "arbitrary"),
        name="norm_matmul",
    )(x, g.reshape(1, d), w)


def _head_rms(x, seg, g):
    cols = []
    for c in range(x.shape[1] // LANES):
        xc = x[:, c * LANES:(c + 1) * LANES]
        ms = _split_dot(xc * xc, seg)
        cols.append(xc * lax.rsqrt(ms + NORM_EPS))
    return jnp.concatenate(cols, axis=1) * g


def _qkv_kernel(x_ref, g_ref, wq_ref, wk_ref, wv_ref, seg_ref, qg_ref, kg_ref,
                q_ref, k_ref, v_ref, *, qk_norm):
    h = _rms_rows(x_ref[...], g_ref[...]).astype(BF16)
    q = _dot(h, wq_ref[...])
    k = _dot(h, wk_ref[...])
    if qk_norm:
        q = _head_rms(q, seg_ref[...], qg_ref[...])
        k = _head_rms(k, seg_ref[...], kg_ref[...])
    q_ref[...] = q
    k_ref[...] = k
    v_ref[...] = _dot(h, wv_ref[...])


def _seg_mean_matrix(n, group):
    r = jnp.arange(n)
    return jnp.where((r[:, None] // group) == (r[None, :] // group), 1.0 / group, 0.0).astype(BF16)


def _qkv(x, g, wq, wk, wv, qg=None, kg=None):
    m, d = x.shape
    tm = _tile(m, 512)
    qk_norm = qg is not None
    if not qk_norm:
        qg = kg = jnp.ones((d,), F32)
    row = lambda i: (i, 0)
    const = lambda i: (0, 0)
    out = jax.ShapeDtypeStruct((m, d), F32)
    return pl.pallas_call(
        functools.partial(_qkv_kernel, qk_norm=qk_norm),
        grid=(m // tm,),
        in_specs=[
            pl.BlockSpec((tm, d), row),
            pl.BlockSpec((1, d), const),
            pl.BlockSpec((d, d), const),
            pl.BlockSpec((d, d), const),
            pl.BlockSpec((d, d), const),
            pl.BlockSpec((LANES, LANES), const),
            pl.BlockSpec((1, d), const),
            pl.BlockSpec((1, d), const),
        ],
        out_specs=[pl.BlockSpec((tm, d), row)] * 3,
        out_shape=[out, out, out],
        compiler_params=_params("parallel"),
        name="qkv_norm" if qk_norm else "qkv",
    )(x, g.reshape(1, d), wq, wk, wv, _seg_mean_matrix(LANES, DIFF_HEAD_DIM),
      qg.reshape(1, d), kg.reshape(1, d))


def _oproj_kernel(o_ref, w_ref, r_ref, y_ref):
    y_ref[...] = r_ref[...] + _dot(o_ref[...], w_ref[...])


def _oproj(o, w, res):
    m, k = o.shape
    d = w.shape[1]
    tm = _tile(m, 512)
    return pl.pallas_call(
        _oproj_kernel,
        grid=(m // tm,),
        in_specs=[
            pl.BlockSpec((tm, k), lambda i: (i, 0)),
            pl.BlockSpec((k, d), lambda i: (0, 0)),
            pl.BlockSpec((tm, d), lambda i: (i, 0)),
        ],
        out_specs=pl.BlockSpec((tm, d), lambda i: (i, 0)),
        out_shape=jax.ShapeDtypeStruct((m, d), F32),
        compiler_params=_params("parallel"),
        name="oproj",
    )(o, w, res)


def _suffix_matrix(n):
    r = jnp.arange(n)
    return (r[:, None] > r[None, :]).astype(BF16)


def _sb_prompt_kernel(q_ref, k_ref, v_ref, u_ref, o_ref, kb_scr, vb_scr, *, tq):
    qi = pl.program_id(2)

    @pl.when(qi == 0)
    def _():
        kb_scr[...] = k_ref[...].astype(BF16)
        vb_scr[...] = v_ref[...].astype(BF16)

    q2 = q_ref[...] * (SB_HEAD_DIM ** -0.5)
    lane = lax.broadcasted_iota(jnp.int32, (1, LANES), 1)
    row = lax.broadcasted_iota(jnp.int32, (tq, tq), 0)
    col = lax.broadcasted_iota(jnp.int32, (tq, tq), 1)
    causal = col < row
    u = u_ref[...]

    def tile(qm, kj, acc, r, masked):
        start = pl.multiple_of(kj * tq, tq)
        kt = kb_scr[pl.ds(start, tq), :]
        vt = vb_scr[pl.ds(start, tq), :]
        z = _dot_nt(qm, kt)
        sp = _softplus(z)
        if masked:
            sp = jnp.where(causal, sp, 0.0)
        cum = _split_dot(sp, u)
        w = jnp.exp(z - sp - cum - r)
        if masked:
            w = jnp.where(causal, w, 0.0)
        acc = acc + _dot(w.astype(BF16), vt)
        r = r + cum[:, :1] + sp[:, :1]
        return acc, r

    outs = []
    for hh in range(2):
        in_head = (lane >= SB_HEAD_DIM * hh) & (lane < SB_HEAD_DIM * (hh + 1))
        qm = jnp.where(in_head, q2, 0.0).astype(BF16)
        acc0 = jnp.zeros((tq, LANES), F32)
        r0 = jnp.zeros((tq, 1), F32)
        acc, r = tile(qm, qi, acc0, r0, True)
        acc, r = lax.fori_loop(
            0, qi, lambda j, c: tile(qm, qi - 1 - j, c[0], c[1], False), (acc, r))
        outs.append(acc)
    o_ref[...] = jnp.where(lane < SB_HEAD_DIM, outs[0], outs[1]).astype(o_ref.dtype)


def _sb_prompt(q, k, v, batch, seq):
    m, d = q.shape
    tq = _tile(seq, 256)
    nq = seq // tq
    pairs = d // LANES
    k3 = k.reshape(batch, seq, d)
    v3 = v.reshape(batch, seq, d)
    return pl.pallas_call(
        functools.partial(_sb_prompt_kernel, tq=tq),
        grid=(batch, pairs, nq),
        in_specs=[
            pl.BlockSpec((tq, LANES), lambda b, p, i: (b * nq + i, p)),
            pl.BlockSpec((None, seq, LANES), lambda b, p, i: (b, 0, p)),
            pl.BlockSpec((None, seq, LANES), lambda b, p, i: (b, 0, p)),
            pl.BlockSpec((tq, tq), lambda b, p, i: (0, 0)),
        ],
        out_specs=pl.BlockSpec((tq, LANES), lambda b, p, i: (b * nq + i, p)),
        out_shape=jax.ShapeDtypeStruct((m, d), BF16),
        scratch_shapes=[pltpu.VMEM((seq, LANES), BF16), pltpu.VMEM((seq, LANES), BF16)],
        compiler_params=_params("parallel", "parallel", "arbitrary"),
        name="sb_prompt",
    )(q, k3, v3, _suffix_matrix(tq))


def _head_rows(x_row, n_rows, width):
    r = lax.broadcasted_iota(jnp.int32, (n_rows, n_rows * width), 0)
    c = lax.broadcasted_iota(jnp.int32, (n_rows, n_rows * width), 1)
    own = (c // width) == r
    return jnp.where(own, x_row, 0.0), own


def _sb_decode_kernel(pt_ref, q_ref, k_ref, v_ref, u_ref, o_ref, qrows_scr, acc_scr, r_scr):
    p = pl.program_id(1)
    d = q_ref.shape[-1]

    @pl.when(p == 0)
    def _():
        q = q_ref[0] * (SB_HEAD_DIM ** -0.5)
        qrows, _ = _head_rows(q, SB_HEADS, SB_HEAD_DIM)
        qrows_scr[...] = qrows.astype(BF16)
        acc_scr[...] = jnp.zeros_like(acc_scr)
        r_scr[...] = jnp.zeros_like(r_scr)

    zt = _dot_nt(qrows_scr[...], k_ref[...].astype(BF16))
    sp = _softplus(zt)
    cum = _split_dot(sp, u_ref[...])
    r = r_scr[...]
    w = jnp.exp(zt - sp - cum - r)
    acc_scr[...] += _dot(w.astype(BF16), v_ref[...].astype(BF16))
    r_scr[...] = r + cum[:, :1] + sp[:, :1]

    @pl.when(p == pl.num_programs(1) - 1)
    def _():
        acc = acc_scr[...]
        _, own = _head_rows(jnp.zeros((1, d), F32), SB_HEADS, SB_HEAD_DIM)
        o_ref[0] = jnp.sum(jnp.where(own, acc, 0.0), axis=0, keepdims=True).astype(o_ref.dtype)


def _sb_decode(q, cache_k, cache_v, layer, page_table):
    r, d = q.shape
    n_pages = page_table.shape[1]
    page = cache_k.shape[2]
    pt = page_table.reshape(-1)
    kv_map = lambda b, p, pt_ref: (layer, pt_ref[b * n_pages + n_pages - 1 - p], 0, 0)
    out = pl.pallas_call(
        _sb_decode_kernel,
        grid_spec=pltpu.PrefetchScalarGridSpec(
            num_scalar_prefetch=1,
            grid=(r, n_pages),
            in_specs=[
                pl.BlockSpec((1, 1, d), lambda b, p, pt_ref: (b, 0, 0)),
                pl.BlockSpec((None, None, page, d), kv_map),
                pl.BlockSpec((None, None, page, d), kv_map),
                pl.BlockSpec((page, page), lambda b, p, pt_ref: (0, 0)),
            ],
            out_specs=pl.BlockSpec((1, 1, d), lambda b, p, pt_ref: (b, 0, 0)),
            scratch_shapes=[pltpu.VMEM((SB_HEADS, d), BF16), pltpu.VMEM((SB_HEADS, d), F32),
                            pltpu.VMEM((SB_HEADS, 1), F32)],
        ),
        out_shape=jax.ShapeDtypeStruct((r, 1, d), BF16),
        compiler_params=_params("parallel", "arbitrary"),
        name="sb_decode",
    )(pt, q.reshape(r, 1, d), cache_k, cache_v, _suffix_matrix(page))
    return out.reshape(r, d)


def _lambda(lq1, lk1, lq2, lk2, lam_init):
    return (jnp.exp(jnp.sum(lq1 * lk1, axis=-1, keepdims=True))
            - jnp.exp(jnp.sum(lq2 * lk2, axis=-1, keepdims=True)) + lam_init)


def _diff_prompt_kernel(slope_ref, q0_ref, q1_ref, k0_ref, k1_ref, v_ref, lam_ref, sg_ref, o_ref,
                        k0_scr, k1_scr, vb_scr, *, tq, lam_init):
    h = pl.program_id(1)
    qi = pl.program_id(2)

    @pl.when(qi == 0)
    def _():
        k0_scr[...] = k0_ref[...].astype(BF16)
        k1_scr[...] = k1_ref[...].astype(BF16)
        vb_scr[...] = v_ref[...].astype(BF16)

    slope = slope_ref[h]
    lane = lax.broadcasted_iota(jnp.int32, (1, LANES), 1)
    in_head = (lane // DIFF_HEAD_DIM) == (h % 2)
    row = lax.broadcasted_iota(jnp.int32, (tq, tq), 0)
    col = lax.broadcasted_iota(jnp.int32, (tq, tq), 1)
    causal = col <= row
    rel = lax.broadcasted_iota(jnp.int32, (1, tq), 1).astype(F32)

    def scores(qm, k_scr, kj):
        start = pl.multiple_of(kj * tq, tq)
        kt = k_scr[pl.ds(start, tq), :]
        bias = slope * (rel + ((kj - qi) * tq).astype(F32))
        return _dot_nt(qm, kt) + bias, vb_scr[pl.ds(start, tq), :]

    def component(q_ref, k_scr):
        qm = jnp.where(in_head, q_ref[...] * (DIFF_HEAD_DIM ** -0.5), 0.0).astype(BF16)
        s, vt = scores(qm, k_scr, qi)
        s = jnp.where(causal, s, -jnp.inf)
        m = jnp.max(s, axis=-1, keepdims=True)
        p = jnp.exp(s - m)
        l = jnp.sum(p, axis=-1, keepdims=True)
        acc = _dot(p.astype(BF16), vt)

        def body(j, c):
            m, l, acc = c
            s, vt = scores(qm, k_scr, j)
            m_new = jnp.maximum(m, jnp.max(s, axis=-1, keepdims=True))
            alpha = jnp.exp(m - m_new)
            p = jnp.exp(s - m_new)
            l = alpha * l + jnp.sum(p, axis=-1, keepdims=True)
            acc = alpha * acc + _dot(p.astype(BF16), vt)
            return m_new, l, acc

        m, l, acc = lax.fori_loop(0, qi, body, (m, l, acc))
        return acc / l

    lam = _lambda(lam_ref[0:1, :], lam_ref[1:2, :], lam_ref[2:3, :], lam_ref[3:4, :], lam_init)
    o = component(q0_ref, k0_scr) - lam * component(q1_ref, k1_scr)
    o = o * lax.rsqrt(jnp.mean(o * o, axis=-1, keepdims=True) + NORM_EPS) * sg_ref[...]
    o_ref[...] = (o * (1.0 - lam_init)).astype(o_ref.dtype)


def _alibi_slopes(n_heads):
    return jnp.exp2(-8.0 * jnp.arange(1, n_heads + 1, dtype=F32) / n_heads)


def _diff_prompt(q, k, v, lam_params, subln_g, lam_init, batch, seq):
    m, d = q.shape
    tq = _tile(seq, 256)
    nq = seq // tq
    half = d // (2 * LANES)
    k3 = k.reshape(batch, seq, d)
    v3 = v.reshape(batch, seq, d)
    return pl.pallas_call(
        functools.partial(_diff_prompt_kernel, tq=tq, lam_init=lam_init),
        grid_spec=pltpu.PrefetchScalarGridSpec(
            num_scalar_prefetch=0,
            grid=(batch, DIFF_HEADS, nq),
            in_specs=[
                pl.BlockSpec(memory_space=pltpu.SMEM),
                pl.BlockSpec((tq, LANES), lambda b, h, i: (b * nq + i, h // 2)),
                pl.BlockSpec((tq, LANES), lambda b, h, i: (b * nq + i, half + h // 2)),
                pl.BlockSpec((None, seq, LANES), lambda b, h, i: (b, 0, h // 2)),
                pl.BlockSpec((None, seq, LANES), lambda b, h, i: (b, 0, half + h // 2)),
                pl.BlockSpec((None, seq, LANES), lambda b, h, i: (b, 0, h)),
                pl.BlockSpec((4, DIFF_HEAD_DIM), lambda b, h, i: (0, 0)),
                pl.BlockSpec((1, LANES), lambda b, h, i: (0, 0)),
            ],
            out_specs=pl.BlockSpec((tq, LANES), lambda b, h, i: (b * nq + i, h)),
            scratch_shapes=[pltpu.VMEM((seq, LANES), BF16)] * 3,
        ),
        out_shape=jax.ShapeDtypeStruct((m, d), BF16),
        compiler_params=_params("parallel", "parallel", "arbitrary"),
        name="diff_prompt",
    )(_alibi_slopes(DIFF_HEADS), q, q, k3, k3, v3, lam_params, subln_g.reshape(1, LANES))


def _diff_decode_kernel(pt_ref, q_ref, kn_ref, vn_ref, k_ref, v_ref, slope_ref, lam_ref, sg_ref, o_ref,
                        qrows_scr, acc_scr, m_scr, l_scr, *, lam_init, past):
    p = pl.program_id(1)
    d = q_ref.shape[-1]
    nrow = 2 * DIFF_HEADS
    page = k_ref.shape[0]

    @pl.when(p == 0)
    def _():
        q = (q_ref[0] * (DIFF_HEAD_DIM ** -0.5)).astype(BF16).astype(F32)
        qrows, _ = _head_rows(q, nrow, DIFF_HEAD_DIM)
        qrows_scr[...] = qrows.astype(BF16)
        kn = kn_ref[0].astype(BF16).astype(F32)
        m_scr[...] = jnp.sum(qrows * kn, axis=-1, keepdims=True)
        l_scr[...] = jnp.ones_like(l_scr)
        acc_scr[...] = jnp.broadcast_to(vn_ref[0].astype(BF16).astype(F32), acc_scr.shape)

    s = _dot_nt(qrows_scr[...], k_ref[...].astype(BF16))
    pos = p * page + lax.broadcasted_iota(jnp.int32, (1, page), 1)
    s = s - slope_ref[...] * (past - pos).astype(F32)
    m = m_scr[...]
    m_new = jnp.maximum(m, jnp.max(s, axis=-1, keepdims=True))
    alpha = jnp.exp(m - m_new)
    pr = jnp.exp(s - m_new)
    l_scr[...] = alpha * l_scr[...] + jnp.sum(pr, axis=-1, keepdims=True)
    acc_scr[...] = alpha * acc_scr[...] + _dot(pr.astype(BF16), v_ref[...].astype(BF16))
    m_scr[...] = m_new

    @pl.when(p == pl.num_programs(1) - 1)
    def _():
        a = acc_scr[...] / l_scr[...]
        lam = _lambda(lam_ref[0:1, :], lam_ref[1:2, :], lam_ref[2:3, :], lam_ref[3:4, :], lam_init)
        dd = a[:DIFF_HEADS] - lam * a[DIFF_HEADS:]
        _, own = _head_rows(jnp.zeros((1, d), F32), DIFF_HEADS, 2 * DIFF_HEAD_DIM)
        dd = jnp.where(own, dd, 0.0)
        ms = jnp.sum(dd * dd, axis=-1, keepdims=True) / (2 * DIFF_HEAD_DIM)
        dn = dd * lax.rsqrt(ms + NORM_EPS)
        o = jnp.sum(dn, axis=0, keepdims=True) * sg_ref[...] * (1.0 - lam_init)
        o_ref[0] = o.astype(o_ref.dtype)


def _diff_decode(q, k_new, v_new, cache_k, cache_v, layer, page_table, lam_params, subln_g, lam_init):
    r, d = q.shape
    n_pages = page_table.shape[1]
    page = cache_k.shape[2]
    pt = page_table.reshape(-1)
    kv_map = lambda b, p, pt_ref: (layer, pt_ref[b * n_pages + p], 0, 0)
    row_map = lambda b, p, pt_ref: (b, 0, 0)
    const = lambda b, p, pt_ref: (0, 0)
    slopes = jnp.tile(_alibi_slopes(DIFF_HEADS), 2).reshape(2 * DIFF_HEADS, 1)
    out = pl.pallas_call(
        functools.partial(_diff_decode_kernel, lam_init=lam_init, past=n_pages * page),
        grid_spec=pltpu.PrefetchScalarGridSpec(
            num_scalar_prefetch=1,
            grid=(r, n_pages),
            in_specs=[
                pl.BlockSpec((1, 1, d), row_map),
                pl.BlockSpec((1, 1, d), row_map),
                pl.BlockSpec((1, 1, d), row_map),
                pl.BlockSpec((None, None, page, d), kv_map),
                pl.BlockSpec((None, None, page, d), kv_map),
                pl.BlockSpec((2 * DIFF_HEADS, 1), const),
                pl.BlockSpec((4, DIFF_HEAD_DIM), const),
                pl.BlockSpec((1, d), const),
            ],
            out_specs=pl.BlockSpec((1, 1, d), row_map),
            scratch_shapes=[pltpu.VMEM((2 * DIFF_HEADS, d), BF16), pltpu.VMEM((2 * DIFF_HEADS, d), F32),
                            pltpu.VMEM((2 * DIFF_HEADS, 1), F32), pltpu.VMEM((2 * DIFF_HEADS, 1), F32)],
        ),
        out_shape=jax.ShapeDtypeStruct((r, 1, d), BF16),
        compiler_params=_params("parallel", "arbitrary"),
        name="diff_decode",
    )(pt, q.reshape(r, 1, d), k_new.reshape(r, 1, d), v_new.reshape(r, 1, d), cache_k, cache_v,
      slopes, lam_params, jnp.tile(subln_g, DIFF_HEADS).reshape(1, d))
    return out.reshape(r, d)


def _expand_matrix(n_in, n_heads, width):
    r = jnp.arange(n_in)[:, None]
    c = jnp.arange(n_heads * width)[None, :]
    return (r == c // width).astype(BF16)


def _split3_dot(x, m):
    hi = x.astype(BF16)
    r1 = x - hi.astype(F32)
    mid = r1.astype(BF16)
    lo = (r1 - mid.astype(F32)).astype(BF16)
    return _dot(hi, m) + _dot(mid, m) + _dot(lo, m)


def _group_rms(y, seg, g, group):
    cols = []
    for c in range(y.shape[1] // group):
        yc = y[:, c * group:(c + 1) * group]
        ms = _split_dot(yc * yc, seg)
        cols.append(yc * lax.rsqrt(ms + NORM_EPS))
    return jnp.concatenate(cols, axis=1) * g


def _ssd_prompt_kernel(z_ref, x_ref, bc_ref, dt_ref, cw_ref, cb_ref, dtb_ref, alog_ref, dskip_ref, ng_ref,
                       tri_ref, e64_ref, e128_ref, seg_ref, y_ref, st_ref, carry_scr, yscr, *, chunk):
    c = pl.program_id(1)
    d_inner = x_ref.shape[1]
    gn = SSM_GROUPS * SSM_STATE

    @pl.when(c == 0)
    def _():
        carry_scr[...] = jnp.zeros_like(carry_scr)
        st_ref[...] = jnp.zeros_like(st_ref)

    def conv(raw, prev, w, b):
        ext = jnp.concatenate([prev, raw], axis=0)
        out = b + w[SSM_CONV - 1:SSM_CONV, :] * raw
        for k in range(1, SSM_CONV):
            shifted = pltpu.roll(ext, k, axis=0)[8:, :]
            out = out + w[SSM_CONV - 1 - k:SSM_CONV - k, :] * shifted
        return _silu(out)

    x_raw = x_ref[...]
    bc_raw = bc_ref[...]
    cw = cw_ref[...]
    cb = cb_ref[...]
    xs = conv(x_raw, carry_scr[:, :d_inner], cw[:, :d_inner], cb[:, :d_inner])
    bcm = conv(bc_raw, carry_scr[:, d_inner:], cw[:, d_inner:], cb[:, d_inner:])
    carry_scr[:, :d_inner] = x_raw[chunk - 8:, :]
    carry_scr[:, d_inner:] = bc_raw[chunk - 8:, :]
    bm = bcm[:, :gn].astype(BF16)
    cm = bcm[:, gn:].astype(BF16)

    dt = _softplus(dt_ref[...] + dtb_ref[...])
    a = dt * (-jnp.exp(alog_ref[...]))
    tri = tri_ref[...]
    acs = _split_dot_left(tri, a)
    acs_t = acs.T
    acs_col = _split3_dot(acs, e128_ref[...])
    acs_x = _split3_dot(acs, e64_ref[...])
    dt_x = _split3_dot(dt, e64_ref[...])
    xd = xs * dt_x
    last_x = acs_x[chunk - 1:chunk, :]
    xdw = (xd * jnp.exp(last_x - acs_x)).astype(BF16)
    xdb = xd.astype(BF16)
    ea_x = jnp.exp(acs_x)

    lrow = lax.broadcasted_iota(jnp.int32, (chunk, chunk), 0)
    scol = lax.broadcasted_iota(jnp.int32, (chunk, chunk), 1)
    tril = scol <= lrow
    lane = lax.broadcasted_iota(jnp.int32, (1, LANES), 1)
    first = lane < SSM_HEAD_DIM
    heads_per_group = SSM_HEADS // SSM_GROUPS

    for pair in range(SSM_HEADS // 2):
        g = (2 * pair) // heads_per_group
        bg = bm[:, g * SSM_STATE:(g + 1) * SSM_STATE]
        cg = cm[:, g * SSM_STATE:(g + 1) * SSM_STATE]
        cbm = _dot_nt(cg, bg)
        xd_pair = xdb[:, pair * LANES:(pair + 1) * LANES]
        ys = []
        for e in range(2):
            hd = 2 * pair + e
            seg = acs_col[:, hd * LANES:(hd + 1) * LANES] - acs_t[hd:hd + 1, :]
            decay = jnp.exp(jnp.where(tril, seg, -jnp.inf))
            ys.append(_dot((cbm * decay).astype(BF16), xd_pair))
        y_diag = jnp.where(first, ys[0], ys[1])
        st_pair = st_ref[2 * pair:2 * pair + 2].reshape(2 * SSM_HEAD_DIM, SSM_STATE)
        y_off = _dot_nt(cg, st_pair.astype(BF16)) * ea_x[:, pair * LANES:(pair + 1) * LANES]
        yscr[:, pair * LANES:(pair + 1) * LANES] = y_diag + y_off
        new = _dot_tn(xdw[:, pair * LANES:(pair + 1) * LANES], bg)
        for e in range(2):
            hd = 2 * pair + e
            dec = jnp.exp(acs_col[chunk - 1:chunk, hd * LANES:(hd + 1) * LANES])
            st_ref[hd] = st_ref[hd] * dec + new[e * SSM_HEAD_DIM:(e + 1) * SSM_HEAD_DIM, :]

    y = yscr[...] + xs * dskip_ref[...]
    y = y * _silu(z_ref[...])
    y_ref[...] = _group_rms(y, seg_ref[...], ng_ref[...], d_inner // SSM_GROUPS).astype(y_ref.dtype)


def _ssd_consts(d_inner):
    r = jnp.arange(SSM_CHUNK)
    tri = (r[:, None] >= r[None, :]).astype(BF16)
    return (tri, _expand_matrix(LANES, SSM_HEADS, SSM_HEAD_DIM), _expand_matrix(LANES, SSM_HEADS, LANES),
            _seg_mean_matrix(d_inner // SSM_GROUPS, d_inner // SSM_GROUPS))


def _pad_lanes(v):
    return jnp.pad(v, (0, LANES - v.shape[0])).reshape(1, LANES)


def _ssd_prompt(zx, conv_w, conv_b, dt_bias, a_log, d_skip, norm_g, batch, seq):
    d_inner = SSM_HEADS * SSM_HEAD_DIM
    chunk = SSM_CHUNK
    nc = seq // chunk
    conv_dim = conv_w.shape[1]
    tri, e64, e128, seg = _ssd_consts(d_inner)
    dt_blk = (2 * d_inner + 2 * SSM_GROUPS * SSM_STATE) // LANES
    const = lambda b, c: (0, 0)
    y, st = pl.pallas_call(
        functools.partial(_ssd_prompt_kernel, chunk=chunk),
        grid=(batch, nc),
        in_specs=[
            pl.BlockSpec((chunk, d_inner), lambda b, c: (b * nc + c, 0)),
            pl.BlockSpec((chunk, d_inner), lambda b, c: (b * nc + c, 1)),
            pl.BlockSpec((chunk, d_inner), lambda b, c: (b * nc + c, 2)),
            pl.BlockSpec((chunk, LANES), lambda b, c: (b * nc + c, dt_blk)),
            pl.BlockSpec((SSM_CONV, conv_dim), const),
            pl.BlockSpec((1, conv_dim), const),
            pl.BlockSpec((1, LANES), const),
            pl.BlockSpec((1, LANES), const),
            pl.BlockSpec((1, d_inner), const),
            pl.BlockSpec((1, d_inner), const),
            pl.BlockSpec(tri.shape, const),
            pl.BlockSpec(e64.shape, const),
            pl.BlockSpec(e128.shape, const),
            pl.BlockSpec(seg.shape, const),
        ],
        out_specs=[
            pl.BlockSpec((chunk, d_inner), lambda b, c: (b * nc + c, 0)),
            pl.BlockSpec((None, SSM_HEADS, SSM_HEAD_DIM, SSM_STATE), lambda b, c: (b, 0, 0, 0)),
        ],
        out_shape=[jax.ShapeDtypeStruct((batch * seq, d_inner), BF16),
                   jax.ShapeDtypeStruct((batch, SSM_HEADS, SSM_HEAD_DIM, SSM_STATE), F32)],
        scratch_shapes=[pltpu.VMEM((8, conv_dim), F32), pltpu.VMEM((chunk, d_inner), F32)],
        compiler_params=_params("parallel", "arbitrary"),
        name="ssd_prompt",
    )(zx, zx, zx, zx, conv_w, conv_b.reshape(1, -1), _pad_lanes(dt_bias), _pad_lanes(a_log),
      jnp.repeat(d_skip, SSM_HEAD_DIM).reshape(1, d_inner), norm_g.reshape(1, d_inner), tri, e64, e128, seg)
    return y, st


def _ssd_decode_kernel(z_ref, x_ref, bc_ref, dt_ref, cs_ref, st_ref, cw_ref, cb_ref, dtb_ref, alog_ref,
                       dskip_ref, ng_ref, e64_ref, e128_ref, seg_ref, y_ref, so_ref, *, rb):
    d_inner = x_ref.shape[1]
    gn = SSM_GROUPS * SSM_STATE
    cw = cw_ref[...]
    xbc_raw = jnp.concatenate([x_ref[...], bc_ref[...]], axis=1)
    out = cb_ref[...] + cw[SSM_CONV - 1:SSM_CONV, :] * xbc_raw
    for j in range(SSM_CONV - 1):
        out = out + cw[j:j + 1, :] * cs_ref[j]
    xbc = _silu(out)
    xs = xbc[:, :d_inner]
    bm = xbc[:, d_inner:d_inner + gn]
    cm = xbc[:, d_inner + gn:]

    dt = _softplus(dt_ref[...] + dtb_ref[...])
    a = dt * (-jnp.exp(alog_ref[...]))
    a_col = _split3_dot(a, e128_ref[...])
    dt_x = _split3_dot(dt, e64_ref[...])
    xd = (xs * dt_x).astype(BF16)
    bmb = bm.astype(BF16)
    cmb = cm.astype(BF16)
    rows = lax.broadcasted_iota(jnp.int32, (rb, 1), 0)
    hpg = SSM_HEADS // SSM_GROUPS
    gw = hpg * SSM_HEAD_DIM

    ycols = []
    for g in range(SSM_GROUPS):
        bg = bmb[:, g * SSM_STATE:(g + 1) * SSM_STATE]
        cg = cmb[:, g * SSM_STATE:(g + 1) * SSM_STATE]
        xg = xd[:, g * gw:(g + 1) * gw]
        yg = jnp.zeros((rb, gw), F32)
        for r in range(rb):
            outer = _dot_tn(jnp.where(rows == r, xg, jnp.zeros_like(xg)), bg)
            news = []
            for e in range(hpg):
                hd = g * hpg + e
                dec = jnp.exp(a_col[r:r + 1, hd * LANES:(hd + 1) * LANES])
                new = st_ref[r, hd] * dec + outer[e * SSM_HEAD_DIM:(e + 1) * SSM_HEAD_DIM, :]
                so_ref[r, hd] = new
                news.append(new)
            yr = _dot_nt(cg, jnp.concatenate(news, axis=0).astype(BF16))
            yg = jnp.where(rows == r, yr, yg)
        ycols.append(yg)
    y = jnp.concatenate(ycols, axis=1) + xs * dskip_ref[...]
    y = y * _silu(z_ref[...])
    y_ref[...] = _group_rms(y, seg_ref[...], ng_ref[...], d_inner // SSM_GROUPS).astype(y_ref.dtype)


def _ssd_decode(zx, conv_state, ssm_state, conv_w, conv_b, dt_bias, a_log, d_skip, norm_g):
    r = zx.shape[0]
    d_inner = SSM_HEADS * SSM_HEAD_DIM
    rb = 8
    conv_dim = conv_w.shape[1]
    _, e64, e128, seg = _ssd_consts(d_inner)
    dt_blk = (2 * d_inner + 2 * SSM_GROUPS * SSM_STATE) // LANES
    cs = jnp.transpose(conv_state, (1, 0, 2))
    const = lambda i: (0, 0)
    st_spec = pl.BlockSpec((rb, SSM_HEADS, SSM_HEAD_DIM, SSM_STATE), lambda i: (i, 0, 0, 0))
    y, st = pl.pallas_call(
        functools.partial(_ssd_decode_kernel, rb=rb),
        grid=(r // rb,),
        in_specs=[
            pl.BlockSpec((rb, d_inner), lambda i: (i, 0)),
            pl.BlockSpec((rb, d_inner), lambda i: (i, 1)),
            pl.BlockSpec((rb, d_inner), lambda i: (i, 2)),
            pl.BlockSpec((rb, LANES), lambda i: (i, dt_blk)),
            pl.BlockSpec((SSM_CONV - 1, rb, conv_dim), lambda i: (0, i, 0)),
            st_spec,
            pl.BlockSpec((SSM_CONV, conv_dim), const),
            pl.BlockSpec((1, conv_dim), const),
            pl.BlockSpec((1, LANES), const),
            pl.BlockSpec((1, LANES), const),
            pl.BlockSpec((1, d_inner), const),
            pl.BlockSpec((1, d_inner), const),
            pl.BlockSpec(e64.shape, const),
            pl.BlockSpec(e128.shape, const),
            pl.BlockSpec(seg.shape, const),
        ],
        out_specs=[pl.BlockSpec((rb, d_inner), lambda i: (i, 0)), st_spec],
        out_shape=[jax.ShapeDtypeStruct((r, d_inner), BF16),
                   jax.ShapeDtypeStruct(ssm_state.shape, F32)],
        compiler_params=_params("parallel"),
        name="ssd_decode",
    )(zx, zx, zx, zx, cs, ssm_state, conv_w, conv_b.reshape(1, -1), _pad_lanes(dt_bias), _pad_lanes(a_log),
      jnp.repeat(d_skip, SSM_HEAD_DIM).reshape(1, d_inner), norm_g.reshape(1, d_inner), e64, e128, seg)
    return y, st


def kernel(x_prompt, x_sample, cache_sb_k, cache_sb_v, cache_diff_k, cache_diff_v, state_ssm_conv, state_ssm,
           page_table, ffn_norm, ffn_w_gate, ffn_w_up, ffn_w_down, mix_norm, sb_w_qkv, sb_w_o,
           diff_w_qkv, diff_q_norm, diff_k_norm, diff_lambda_q1, diff_lambda_k1, diff_lambda_q2, diff_lambda_k2,
           diff_subln, diff_w_o, ssm_w_in, ssm_conv_w, ssm_conv_b, ssm_dt_bias, ssm_a_log, ssm_d, ssm_norm,
           ssm_w_out):
    bp, seq, d = x_prompt.shape
    bs = x_sample.shape[0]
    depth = ffn_norm.shape[0]
    n_pool, page = cache_sb_k.shape[1], cache_sb_k.shape[2]
    xp = x_prompt.reshape(bp * seq, d)
    xs = x_sample.reshape(bs, d)
    csk = cache_sb_k.reshape(cache_sb_k.shape[0], n_pool, page, d)
    csv = cache_sb_v.reshape(cache_sb_v.shape[0], n_pool, page, d)
    cdk = cache_diff_k.reshape(cache_diff_k.shape[0], n_pool, page, d)
    cdv = cache_diff_v.reshape(cache_diff_v.shape[0], n_pool, page, d)
    wg, wu, wd = ffn_w_gate.astype(BF16), ffn_w_up.astype(BF16), ffn_w_down.astype(BF16)

    sb_kp, sb_vp, sb_ks, sb_vs = [], [], [], []
    d_kp, d_vp, d_ks, d_vs = [], [], [], []
    cv_p, ss_p, cv_s, ss_s = [], [], [], []
    for i in range(depth):
        kind, j = i % N_MIXERS, i // N_MIXERS
        xp = _ffn(xp, ffn_norm[i, 0], wg[i, 0], wu[i, 0], wd[i, 0])
        xs = _ffn(xs, ffn_norm[i, 0], wg[i, 0], wu[i, 0], wd[i, 0])
        if kind == 0:
            w = sb_w_qkv[j].astype(BF16)
            wq, wk, wv = w[:, :d], w[:, d:2 * d], w[:, 2 * d:]
            wo = sb_w_o[j].astype(BF16)
            q, k, v = _qkv(xp, mix_norm[i], wq, wk, wv)
            sb_kp.append(k.reshape(bp, seq, SB_HEADS, SB_HEAD_DIM))
            sb_vp.append(v.reshape(bp, seq, SB_HEADS, SB_HEAD_DIM))
            xp = _oproj(_sb_prompt(q, k, v, bp, seq), wo, xp)
            q, k, v = _qkv(xs, mix_norm[i], wq, wk, wv)
            sb_ks.append(k.reshape(bs, 1, SB_HEADS, SB_HEAD_DIM))
            sb_vs.append(v.reshape(bs, 1, SB_HEADS, SB_HEAD_DIM))
            xs = _oproj(_sb_decode(q, csk, csv, j, page_table), wo, xs)
        elif kind == 1:
            lam_init = 0.8 - 0.6 * math.exp(-0.3 * i)
            w = diff_w_qkv[j].astype(BF16)
            wq, wk, wv = w[:, :d], w[:, d:2 * d], w[:, 2 * d:]
            wo = diff_w_o[j].astype(BF16)
            qg = jnp.tile(diff_q_norm[j], d // DIFF_HEAD_DIM)
            kg = jnp.tile(diff_k_norm[j], d // DIFF_HEAD_DIM)
            lam_params = jnp.stack([diff_lambda_q1[j], diff_lambda_k1[j], diff_lambda_q2[j], diff_lambda_k2[j]])
            q, k, v = _qkv(xp, mix_norm[i], wq, wk, wv, qg, kg)
            d_kp.append(k.reshape(bp, seq, 2, DIFF_HEADS, DIFF_HEAD_DIM))
            d_vp.append(v.reshape(bp, seq, DIFF_HEADS, 2 * DIFF_HEAD_DIM))
            o = _diff_prompt(q, k, v, lam_params, diff_subln[j], lam_init, bp, seq)
            xp = _oproj(o, wo, xp)
            q, k, v = _qkv(xs, mix_norm[i], wq, wk, wv, qg, kg)
            d_ks.append(k.reshape(bs, 1, 2, DIFF_HEADS, DIFF_HEAD_DIM))
            d_vs.append(v.reshape(bs, 1, DIFF_HEADS, 2 * DIFF_HEAD_DIM))
            o = _diff_decode(q, k, v, cdk, cdv, j, page_table, lam_params, diff_subln[j], lam_init)
            xs = _oproj(o, wo, xs)
        else:
            d_inner = SSM_HEADS * SSM_HEAD_DIM
            conv_dim = ssm_conv_w.shape[2]
            w_in = jnp.pad(ssm_w_in[j], ((0, 0), (0, LANES - SSM_HEADS))).astype(BF16)
            wo = ssm_w_out[j].astype(BF16)
            sp = (ssm_conv_w[j], ssm_conv_b[j], ssm_dt_bias[j], ssm_a_log[j], ssm_d[j], ssm_norm[j])
            tn = _tile(w_in.shape[1], 896)
            zx = _norm_matmul(xp, mix_norm[i], w_in, tn)
            y, st = _ssd_prompt(zx, *sp, bp, seq)
            cv_p.append(zx.reshape(bp, seq, -1)[:, seq - (SSM_CONV - 1):, d_inner:d_inner + conv_dim])
            ss_p.append(st)
            xp = _oproj(y, wo, xp)
            zx = _norm_matmul(xs, mix_norm[i], w_in, tn)
            y, st = _ssd_decode(zx, state_ssm_conv[j], state_ssm[j], *sp)
            cv_s.append(jnp.concatenate([state_ssm_conv[j][:, 1:], zx[:, None, d_inner:d_inner + conv_dim]], axis=1))
            ss_s.append(st)
            xs = _oproj(y, wo, xs)
        xp = _ffn(xp, ffn_norm[i, 1], wg[i, 1], wu[i, 1], wd[i, 1])
        xs = _ffn(xs, ffn_norm[i, 1], wg[i, 1], wu[i, 1], wd[i, 1])
    return (xp.reshape(bp, seq, d), xs.reshape(bs, 1, d),
            jnp.stack(sb_kp), jnp.stack(sb_vp), jnp.stack(sb_ks), jnp.stack(sb_vs),
            jnp.stack(d_kp), jnp.stack(d_vp), jnp.stack(d_ks), jnp.stack(d_vs),
            jnp.stack(cv_p), jnp.stack(ss_p), jnp.stack(cv_s), jnp.stack(ss_s))
```

```python
import functools
import math

import jax
import jax.numpy as jnp
from jax import lax
from jax.experimental import pallas as pl
from jax.experimental.pallas import tpu as pltpu

F32 = jnp.float32
BF16 = jnp.bfloat16

NORM_EPS = 1e-6
N_MIXERS = 3
SB_HEADS = 16
SB_HEAD_DIM = 64
DIFF_HEADS = 8
DIFF_HEAD_DIM = 64
SSM_HEAD_DIM = 64
SSM_HEADS = 32
SSM_GROUPS = 8
SSM_STATE = 128
SSM_CONV = 4
SSM_CHUNK = 128
LANES = 128
VMEM_LIMIT = 48 * 1024 * 1024


def _params(*sem):
    return pltpu.CompilerParams(dimension_semantics=sem, vmem_limit_bytes=VMEM_LIMIT)


def _tile(n, pref):
    t = min(n, pref)
    while n % t:
        t //= 2
    return t


def _dot(a, b):
    return jnp.dot(a, b, preferred_element_type=F32)


def _dot_nt(a, b):
    return lax.dot_general(a, b, (((1,), (1,)), ((), ())), preferred_element_type=F32)


def _dot_tn(a, b):
    return lax.dot_general(a, b, (((0,), (0,)), ((), ())), preferred_element_type=F32)


def _split_dot(x, m):
    hi = x.astype(BF16)
    lo = (x - hi.astype(F32)).astype(BF16)
    return _dot(hi, m) + _dot(lo, m)


def _split_dot_left(m, x):
    hi = x.astype(BF16)
    lo = (x - hi.astype(F32)).astype(BF16)
    return _dot(m, hi) + _dot(m, lo)


def _rms_rows(x, g):
    return x * lax.rsqrt(jnp.mean(x * x, axis=-1, keepdims=True) + NORM_EPS) * g


def _silu(x):
    return x / (1.0 + jnp.exp(-x))


def _softplus(z):
    return jnp.maximum(z, 0.0) + jnp.log(1.0 + jnp.exp(-jnp.abs(z)))


def _ffn_kernel(x_ref, g_ref, wg_ref, wu_ref, wd_ref, o_ref, h_scr, acc_scr):
    f = pl.program_id(1)

    @pl.when(f == 0)
    def _():
        h_scr[...] = _rms_rows(x_ref[...], g_ref[...]).astype(BF16)
        acc_scr[...] = jnp.zeros_like(acc_scr)

    h = h_scr[...]
    gate = _dot(h, wg_ref[...])
    up = _dot(h, wu_ref[...])
    act = (_silu(gate) * up).astype(BF16)
    acc_scr[...] += _dot(act, wd_ref[...])

    @pl.when(f == pl.num_programs(1) - 1)
    def _():
        o_ref[...] = x_ref[...] + 0.5 * acc_scr[...]


def _ffn(x, g, wg, wu, wd):
    m, d = x.shape
    dff = wg.shape[1]
    tm = _tile(m, 1024)
    tf = _tile(dff, 256)
    return pl.pallas_call(
        _ffn_kernel,
        grid=(m // tm, dff // tf),
        in_specs=[
            pl.BlockSpec((tm, d), lambda i, f: (i, 0)),
            pl.BlockSpec((1, d), lambda i, f: (0, 0)),
            pl.BlockSpec((d, tf), lambda i, f: (0, f)),
            pl.BlockSpec((d, tf), lambda i, f: (0, f)),
            pl.BlockSpec((tf, d), lambda i, f: (f, 0)),
        ],
        out_specs=pl.BlockSpec((tm, d), lambda i, f: (i, 0)),
        out_shape=jax.ShapeDtypeStruct((m, d), F32),
        scratch_shapes=[pltpu.VMEM((tm, d), BF16), pltpu.VMEM((tm, d), F32)],
        compiler_params=_params("parallel", "arbitrary"),
        name="ffn",
    )(x, g.reshape(1, d), wg, wu, wd)


def _norm_matmul_kernel(x_ref, g_ref, w_ref, o_ref, h_scr):
    @pl.when(pl.program_id(1) == 0)
    def _():
        h_scr[...] = _rms_rows(x_ref[...], g_ref[...]).astype(BF16)

    o_ref[...] = _dot(h_scr[...], w_ref[...])


def _norm_matmul(x, g, w, tn):
    m, d = x.shape
    n = w.shape[1]
    tm = _tile(m, 512)
    return pl.pallas_call(
        _norm_matmul_kernel,
        grid=(m // tm, n // tn),
        in_specs=[
            pl.BlockSpec((tm, d), lambda i, j: (i, 0)),
            pl.BlockSpec((1, d), lambda i, j: (0, 0)),
            pl.BlockSpec((d, tn), lambda i, j: (0, j)),
        ],
        out_specs=pl.BlockSpec((tm, tn), lambda i, j: (i, j)),
        out_shape=jax.ShapeDtypeStruct((m, n), F32),
        scratch_shapes=[pltpu.VMEM((tm, d), BF16)],
        compiler_params=_params("parallel", "arbitrary"),
        name="norm_matmul",
    )(x, g.reshape(1, d), w)


def _head_rms(x, seg, g):
    cols = []
    for c in range(x.shape[1] // LANES):
        xc = x[:, c * LANES:(c + 1) * LANES]
        ms = _split_dot(xc * xc, seg)
        cols.append(xc * lax.rsqrt(ms + NORM_EPS))
    return jnp.concatenate(cols, axis=1) * g


def _qkv_kernel(x_ref, g_ref, wq_ref, wk_ref, wv_ref, seg_ref, qg_ref, kg_ref,
                q_ref, k_ref, v_ref, *, qk_norm):
    h = _rms_rows(x_ref[...], g_ref[...]).astype(BF16)
    q = _dot(h, wq_ref[...])
    k = _dot(h, wk_ref[...])
    if qk_norm:
        q = _head_rms(q, seg_ref[...], qg_ref[...])
        k = _head_rms(k, seg_ref[...], kg_ref[...])
    q_ref[...] = q
    k_ref[...] = k
    v_ref[...] = _dot(h, wv_ref[...])


def _seg_mean_matrix(n, group):
    r = jnp.arange(n)
    return jnp.where((r[:, None] // group) == (r[None, :] // group), 1.0 / group, 0.0).astype(BF16)


def _qkv(x, g, wq, wk, wv, qg=None, kg=None):
    m, d = x.shape
    tm = _tile(m, 512)
    qk_norm = qg is not None
    if not qk_norm:
        qg = kg = jnp.ones((d,), F32)
    row = lambda i: (i, 0)
    const = lambda i: (0, 0)
    out = jax.ShapeDtypeStruct((m, d), F32)
    return pl.pallas_call(
        functools.partial(_qkv_kernel, qk_norm=qk_norm),
        grid=(m // tm,),
        in_specs=[
            pl.BlockSpec((tm, d), row),
            pl.BlockSpec((1, d), const),
            pl.BlockSpec((d, d), const),
            pl.BlockSpec((d, d), const),
            pl.BlockSpec((d, d), const),
            pl.BlockSpec((LANES, LANES), const),
            pl.BlockSpec((1, d), const),
            pl.BlockSpec((1, d), const),
        ],
        out_specs=[pl.BlockSpec((tm, d), row)] * 3,
        out_shape=[out, out, out],
        compiler_params=_params("parallel"),
        name="qkv_norm" if qk_norm else "qkv",
    )(x, g.reshape(1, d), wq, wk, wv, _seg_mean_matrix(LANES, DIFF_HEAD_DIM),
      qg.reshape(1, d), kg.reshape(1, d))


def _oproj_kernel(o_ref, w_ref, r_ref, y_ref):
    y_ref[...] = r_ref[...] + _dot(o_ref[...].astype(BF16), w_ref[...])


def _oproj(o, w, res):
    m, k = o.shape
    d = w.shape[1]
    tm = _tile(m, 512)
    return pl.pallas_call(
        _oproj_kernel,
        grid=(m // tm,),
        in_specs=[
            pl.BlockSpec((tm, k), lambda i: (i, 0)),
            pl.BlockSpec((k, d), lambda i: (0, 0)),
            pl.BlockSpec((tm, d), lambda i: (i, 0)),
        ],
        out_specs=pl.BlockSpec((tm, d), lambda i: (i, 0)),
        out_shape=jax.ShapeDtypeStruct((m, d), F32),
        compiler_params=_params("parallel"),
        name="oproj",
    )(o, w, res)


def _suffix_matrix(n):
    r = jnp.arange(n)
    return (r[:, None] > r[None, :]).astype(BF16)


def _sb_prompt_kernel(q_ref, k_ref, v_ref, u_ref, o_ref, kb_scr, vb_scr, acc_scr, *, tk, halves):
    qi = pl.program_id(2)

    @pl.when(qi == 0)
    def _():
        kb_scr[...] = k_ref[...].astype(BF16)
        vb_scr[...] = v_ref[...].astype(BF16)

    q2 = q_ref[...] * (SB_HEAD_DIM ** -0.5)
    lane = lax.broadcasted_iota(jnp.int32, (1, LANES), 1)
    row = lax.broadcasted_iota(jnp.int32, (tk, tk), 0)
    col = lax.broadcasted_iota(jnp.int32, (tk, tk), 1)
    causal = col < row
    u = u_ref[...]
    base = qi * halves
    chains = [(hh, a) for hh in range(2) for a in range(halves)]
    qms = []
    for hh, a in chains:
        in_head = (lane // SB_HEAD_DIM) == hh
        qms.append(jnp.where(in_head, q2[a * tk:(a + 1) * tk], 0.0).astype(BF16))

    def tiles(cs, kjs, rs, masked):
        starts = [pl.multiple_of(kj * tk, tk) for kj in kjs]
        zs = [_dot_nt(qms[c], kb_scr[pl.ds(st, tk), :]) for c, st in zip(cs, starts)]
        sps = [_softplus(z) for z in zs]
        if masked:
            sps = [jnp.where(causal, sp, 0.0) for sp in sps]
        cums = []
        for sp in sps:
            hi = sp.astype(BF16)
            lo = (sp - hi.astype(F32)).astype(BF16)
            both = _dot(jnp.concatenate([hi, lo], axis=0), u)
            cums.append(both[:tk] + both[tk:])
        ws = [jnp.exp(z - sp - cum - r) for z, sp, cum, r in zip(zs, sps, cums, rs)]
        if masked:
            ws = [jnp.where(causal, w, 0.0) for w in ws]
        for c, st, w in zip(cs, starts, ws):
            pv = _dot(w.astype(BF16), vb_scr[pl.ds(st, tk), :])
            if masked:
                acc_scr[c] = pv
            else:
                acc_scr[c] += pv
        return [r + cum[:, :1] + sp[:, :1] for r, cum, sp in zip(rs, cums, sps)]

    n_chain = len(chains)
    rs = [jnp.zeros((tk, 1), F32)] * n_chain
    for t in range(halves):
        cs = [c for c in range(n_chain) if chains[c][1] >= t]
        new = tiles(cs, [base + chains[c][1] - t for c in cs], [rs[c] for c in cs], t == 0)
        for c, r in zip(cs, new):
            rs[c] = r

    def body(j, rs):
        return tuple(tiles(list(range(n_chain)), [base - 1 - j] * n_chain, list(rs), False))

    lax.fori_loop(0, base, body, tuple(rs))
    for a in range(halves):
        o_ref[a * tk:(a + 1) * tk, :] = jnp.where(
            lane < SB_HEAD_DIM, acc_scr[a], acc_scr[halves + a]).astype(o_ref.dtype)


ATTN_TK = 256
ATTN_HALVES = 2


def _attn_tiles(seq):
    tk = _tile(seq, ATTN_TK)
    halves = ATTN_HALVES if seq % (ATTN_HALVES * tk) == 0 else 1
    return tk, halves


def _sb_prompt(q, k, v, batch, seq):
    m, d = q.shape
    tk, halves = _attn_tiles(seq)
    tq = tk * halves
    nq = seq // tq
    pairs = d // LANES
    k3 = k.reshape(batch, seq, d)
    v3 = v.reshape(batch, seq, d)
    return pl.pallas_call(
        functools.partial(_sb_prompt_kernel, tk=tk, halves=halves),
        grid=(batch, pairs, nq),
        in_specs=[
            pl.BlockSpec((tq, LANES), lambda b, p, i: (b * nq + i, p)),
            pl.BlockSpec((None, seq, LANES), lambda b, p, i: (b, 0, p)),
            pl.BlockSpec((None, seq, LANES), lambda b, p, i: (b, 0, p)),
            pl.BlockSpec((tk, tk), lambda b, p, i: (0, 0)),
        ],
        out_specs=pl.BlockSpec((tq, LANES), lambda b, p, i: (b * nq + i, p)),
        out_shape=jax.ShapeDtypeStruct((m, d), BF16),
        scratch_shapes=[pltpu.VMEM((seq, LANES), BF16), pltpu.VMEM((seq, LANES), BF16),
                        pltpu.VMEM((2 * halves, tk, LANES), F32)],
        compiler_params=_params("parallel", "parallel", "arbitrary"),
        name="sb_prompt",
    )(q, k3, v3, _suffix_matrix(tk))


DECODE_PAGES_PER_STEP = 2


def _pages_per_step(n_pages):
    return DECODE_PAGES_PER_STEP if n_pages % DECODE_PAGES_PER_STEP == 0 else 1


def _head_slab(ref, i, n_rows):
    return ref[pl.ds(i, ref.shape[0] // n_rows, stride=n_rows), :].astype(BF16)


def _head_scores(q, k_ref, n_rows):
    page = k_ref.shape[0] // n_rows
    row = lax.broadcasted_iota(jnp.int32, (n_rows, page), 0)
    s = jnp.zeros((n_rows, page), F32)
    for i in range(n_rows):
        s = jnp.where(row == i, _dot_nt(q, _head_slab(k_ref, i, n_rows)), s)
    return s


def _sb_decode_kernel(pt_ref, q_ref, *refs, pps):
    k_refs, v_refs = refs[:pps], refs[pps:2 * pps]
    u_ref, o_ref, q_scr, acc_scr, r_scr = refs[2 * pps:]
    p = pl.program_id(1)

    @pl.when(p == 0)
    def _():
        q_scr[...] = (q_ref[0] * (SB_HEAD_DIM ** -0.5)).astype(BF16)
        acc_scr[...] = jnp.zeros_like(acc_scr)
        r_scr[...] = jnp.zeros_like(r_scr)

    q = q_scr[...]
    rowd = lax.broadcasted_iota(jnp.int32, (SB_HEADS, SB_HEAD_DIM), 0)
    r = r_scr[...]
    acc = acc_scr[...]
    for k_ref, v_ref in zip(k_refs, v_refs):
        zt = _head_scores(q, k_ref, SB_HEADS)
        sp = _softplus(zt)
        cum = _split_dot(sp, u_ref[...])
        wb = jnp.exp(zt - sp - cum - r).astype(BF16)
        for h in range(SB_HEADS):
            pv = _dot(wb, _head_slab(v_ref, h, SB_HEADS))
            acc = acc + jnp.where(rowd == h, pv, 0.0)
        r = r + cum[:, :1] + sp[:, :1]
    acc_scr[...] = acc
    r_scr[...] = r

    @pl.when(p == pl.num_programs(1) - 1)
    def _():
        o_ref[0] = acc


def _sb_decode(q, cache_k, cache_v, layer, page_table):
    r, d = q.shape
    n_pages = page_table.shape[1]
    page = cache_k.shape[2]
    pps = _pages_per_step(n_pages)
    pt = page_table.reshape(-1)

    flat = cache_k.shape[:2] + (page * SB_HEADS, SB_HEAD_DIM)
    cache_k, cache_v = cache_k.reshape(flat), cache_v.reshape(flat)

    def kv_map(i):
        return lambda b, p, pt_ref: (layer, pt_ref[b * n_pages + n_pages - 1 - (p * pps + i)], 0, 0)

    kv_specs = [pl.BlockSpec((None, None, page * SB_HEADS, SB_HEAD_DIM), kv_map(i)) for i in range(pps)]
    row_map = lambda b, p, pt_ref: (b, 0, 0)
    out = pl.pallas_call(
        functools.partial(_sb_decode_kernel, pps=pps),
        grid_spec=pltpu.PrefetchScalarGridSpec(
            num_scalar_prefetch=1,
            grid=(r, n_pages // pps),
            in_specs=[pl.BlockSpec((1, SB_HEADS, SB_HEAD_DIM), row_map)] + kv_specs + kv_specs
            + [pl.BlockSpec((page, page), lambda b, p, pt_ref: (0, 0))],
            out_specs=pl.BlockSpec((1, SB_HEADS, SB_HEAD_DIM), row_map),
            scratch_shapes=[pltpu.VMEM((SB_HEADS, SB_HEAD_DIM), BF16), pltpu.VMEM((SB_HEADS, SB_HEAD_DIM), F32),
                            pltpu.VMEM((SB_HEADS, 1), F32)],
        ),
        out_shape=jax.ShapeDtypeStruct((r, SB_HEADS, SB_HEAD_DIM), F32),
        compiler_params=_params("parallel", "arbitrary"),
        name="sb_decode",
    )(pt, q.reshape(r, SB_HEADS, SB_HEAD_DIM), *([cache_k] * pps), *([cache_v] * pps), _suffix_matrix(page))
    return out.reshape(r, d)


def _lambda(lq1, lk1, lq2, lk2, lam_init):
    return (jnp.exp(jnp.sum(lq1 * lk1, axis=-1, keepdims=True))
            - jnp.exp(jnp.sum(lq2 * lk2, axis=-1, keepdims=True)) + lam_init)


def _diff_prompt_kernel(slope_ref, q0_ref, q1_ref, k0_ref, k1_ref, v_ref, lam_ref, sg_ref, o_ref,
                        k0_scr, k1_scr, vb_scr, acc_scr, *, tk, halves, lam_init):
    h = pl.program_id(1)
    qi = pl.program_id(2)

    @pl.when(qi == 0)
    def _():
        k0_scr[...] = k0_ref[...].astype(BF16)
        k1_scr[...] = k1_ref[...].astype(BF16)
        vb_scr[...] = v_ref[...].astype(BF16)

    slope = slope_ref[h]
    lane = lax.broadcasted_iota(jnp.int32, (1, LANES), 1)
    in_head = (lane // DIFF_HEAD_DIM) == (h % 2)
    row = lax.broadcasted_iota(jnp.int32, (tk, tk), 0)
    col = lax.broadcasted_iota(jnp.int32, (tk, tk), 1)
    causal = col <= row
    rel = lax.broadcasted_iota(jnp.int32, (1, tk), 1).astype(F32)
    base = qi * halves
    k_scrs = (k0_scr, k1_scr)
    chains = [(c, a) for c in range(2) for a in range(halves)]
    qms = []
    for c, a in chains:
        q = (q0_ref, q1_ref)[c][a * tk:(a + 1) * tk, :]
        qms.append(jnp.where(in_head, q * (DIFF_HEAD_DIM ** -0.5), 0.0).astype(BF16))

    def tiles(cs, kjs, states, diag):
        starts = [pl.multiple_of(kj * tk, tk) for kj in kjs]
        ss = [_dot_nt(qms[ci], k_scrs[chains[ci][0]][pl.ds(st, tk), :])
              + slope * (rel + ((kj - base) * tk).astype(F32)) for ci, st, kj in zip(cs, starts, kjs)]
        if diag:
            ss = [jnp.where(causal, s, -jnp.inf) for s in ss]
            ms = [jnp.max(s, axis=-1, keepdims=True) for s in ss]
        else:
            ms = [jnp.maximum(m, jnp.max(s, axis=-1, keepdims=True)) for (m, _), s in zip(states, ss)]
        ps = [jnp.exp(s - m) for s, m in zip(ss, ms)]
        out = []
        for i, (ci, st, p) in enumerate(zip(cs, starts, ps)):
            pv = _dot(p.astype(BF16), vb_scr[pl.ds(st, tk), :])
            psum = jnp.sum(p, axis=-1, keepdims=True)
            if diag:
                acc_scr[ci] = pv
                out.append((ms[i], psum))
            else:
                alpha = jnp.exp(states[i][0] - ms[i])
                acc_scr[ci] = alpha * acc_scr[ci] + pv
                out.append((ms[i], alpha * states[i][1] + psum))
        return out

    n_chain = len(chains)
    states = [None] * n_chain
    for t in range(halves):
        cs = [ci for ci in range(n_chain) if chains[ci][1] >= t]
        new = tiles(cs, [base + chains[ci][1] - t for ci in cs], [states[ci] for ci in cs], t == 0)
        for ci, st in zip(cs, new):
            states[ci] = st

    def body(j, states):
        return tuple(tiles(list(range(n_chain)), [j] * n_chain, list(states), False))

    states = lax.fori_loop(0, base, body, tuple(states))
    lam = _lambda(lam_ref[0:1, :], lam_ref[1:2, :], lam_ref[2:3, :], lam_ref[3:4, :], lam_init)
    for a in range(halves):
        o = acc_scr[a] / states[a][1] - lam * (acc_scr[halves + a] / states[halves + a][1])
        o = o * lax.rsqrt(jnp.mean(o * o, axis=-1, keepdims=True) + NORM_EPS) * sg_ref[...]
        o_ref[a * tk:(a + 1) * tk, :] = (o * (1.0 - lam_init)).astype(o_ref.dtype)


def _alibi_slopes(n_heads):
    return jnp.exp2(-8.0 * jnp.arange(1, n_heads + 1, dtype=F32) / n_heads)


def _diff_prompt(q, k, v, lam_params, subln_g, lam_init, batch, seq):
    m, d = q.shape
    tk, halves = _attn_tiles(seq)
    tq = tk * halves
    nq = seq // tq
    half = d // (2 * LANES)
    k3 = k.reshape(batch, seq, d)
    v3 = v.reshape(batch, seq, d)
    return pl.pallas_call(
        functools.partial(_diff_prompt_kernel, tk=tk, halves=halves, lam_init=lam_init),
        grid_spec=pltpu.PrefetchScalarGridSpec(
            num_scalar_prefetch=0,
            grid=(batch, DIFF_HEADS, nq),
            in_specs=[
                pl.BlockSpec(memory_space=pltpu.SMEM),
                pl.BlockSpec((tq, LANES), lambda b, h, i: (b * nq + i, h // 2)),
                pl.BlockSpec((tq, LANES), lambda b, h, i: (b * nq + i, half + h // 2)),
                pl.BlockSpec((None, seq, LANES), lambda b, h, i: (b, 0, h // 2)),
                pl.BlockSpec((None, seq, LANES), lambda b, h, i: (b, 0, half + h // 2)),
                pl.BlockSpec((None, seq, LANES), lambda b, h, i: (b, 0, h)),
                pl.BlockSpec((4, DIFF_HEAD_DIM), lambda b, h, i: (0, 0)),
                pl.BlockSpec((1, LANES), lambda b, h, i: (0, 0)),
            ],
            out_specs=pl.BlockSpec((tq, LANES), lambda b, h, i: (b * nq + i, h)),
            scratch_shapes=[pltpu.VMEM((seq, LANES), BF16)] * 3 + [pltpu.VMEM((2 * halves, tk, LANES), F32)],
        ),
        out_shape=jax.ShapeDtypeStruct((m, d), BF16),
        compiler_params=_params("parallel", "parallel", "arbitrary"),
        name="diff_prompt",
    )(_alibi_slopes(DIFF_HEADS), q, q, k3, k3, v3, lam_params, subln_g.reshape(1, LANES))


def _diff_decode_kernel(pt_ref, q_ref, kn_ref, vn_ref, *refs, pps, lam_init, past):
    k_refs, v_refs = refs[:pps], refs[pps:2 * pps]
    slope_ref, lam_ref, sg_ref, o_ref, q_scr, acc_scr, m_scr, l_scr = refs[2 * pps:]
    p = pl.program_id(1)
    nrow = 2 * DIFF_HEADS
    page = k_refs[0].shape[0] // nrow

    @pl.when(p == 0)
    def _():
        qb = (q_ref[0] * (DIFF_HEAD_DIM ** -0.5)).astype(BF16)
        q_scr[...] = qb
        kn = kn_ref[0].astype(BF16).astype(F32)
        m_scr[...] = jnp.sum(qb.astype(F32) * kn, axis=-1, keepdims=True)
        l_scr[...] = jnp.ones_like(l_scr)
        vn = vn_ref[0].astype(BF16).astype(F32)
        acc_scr[...] = jnp.concatenate([vn, vn], axis=0)

    q = q_scr[...]
    rowv = lax.broadcasted_iota(jnp.int32, (nrow, 2 * DIFF_HEAD_DIM), 0)
    m, l, acc = m_scr[...], l_scr[...], acc_scr[...]
    for i, (k_ref, v_ref) in enumerate(zip(k_refs, v_refs)):
        s = _head_scores(q, k_ref, nrow)
        pos = (p * pps + i) * page + lax.broadcasted_iota(jnp.int32, (1, page), 1)
        s = s - slope_ref[...] * (past - pos).astype(F32)
        m_new = jnp.maximum(m, jnp.max(s, axis=-1, keepdims=True))
        alpha = jnp.exp(m - m_new)
        pr = jnp.exp(s - m_new)
        l = alpha * l + jnp.sum(pr, axis=-1, keepdims=True)
        prb = pr.astype(BF16)
        acc = alpha * acc
        for h in range(DIFF_HEADS):
            pv = _dot(prb, _head_slab(v_ref, h, DIFF_HEADS))
            acc = acc + jnp.where((rowv % DIFF_HEADS) == h, pv, 0.0)
        m = m_new
    m_scr[...] = m
    l_scr[...] = l
    acc_scr[...] = acc

    @pl.when(p == pl.num_programs(1) - 1)
    def _():
        a = acc / l
        lam = _lambda(lam_ref[0:1, :], lam_ref[1:2, :], lam_ref[2:3, :], lam_ref[3:4, :], lam_init)
        dd = a[:DIFF_HEADS] - lam * a[DIFF_HEADS:]
        dn = dd * lax.rsqrt(jnp.mean(dd * dd, axis=-1, keepdims=True) + NORM_EPS)
        o_ref[0] = dn * sg_ref[...] * (1.0 - lam_init)


def _diff_decode(q, k_new, v_new, cache_k, cache_v, layer, page_table, lam_params, subln_g, lam_init):
    r, d = q.shape
    n_pages = page_table.shape[1]
    page = cache_k.shape[2]
    pps = _pages_per_step(n_pages)
    nrow = 2 * DIFF_HEADS
    vdim = 2 * DIFF_HEAD_DIM
    pt = page_table.reshape(-1)

    def page_of(i):
        return lambda b, p, pt_ref: pt_ref[b * n_pages + p * pps + i]

    cache_k = cache_k.reshape(cache_k.shape[:2] + (page * nrow, DIFF_HEAD_DIM))
    cache_v = cache_v.reshape(cache_v.shape[:2] + (page * DIFF_HEADS, vdim))
    k_specs = [pl.BlockSpec((None, None, page * nrow, DIFF_HEAD_DIM),
                            lambda b, p, pt_ref, f=page_of(i): (layer, f(b, p, pt_ref), 0, 0))
               for i in range(pps)]
    v_specs = [pl.BlockSpec((None, None, page * DIFF_HEADS, vdim),
                            lambda b, p, pt_ref, f=page_of(i): (layer, f(b, p, pt_ref), 0, 0))
               for i in range(pps)]
    row_map = lambda b, p, pt_ref: (b, 0, 0)
    const = lambda b, p, pt_ref: (0, 0)
    slopes = jnp.tile(_alibi_slopes(DIFF_HEADS), 2).reshape(nrow, 1)
    out = pl.pallas_call(
        functools.partial(_diff_decode_kernel, pps=pps, lam_init=lam_init, past=n_pages * page),
        grid_spec=pltpu.PrefetchScalarGridSpec(
            num_scalar_prefetch=1,
            grid=(r, n_pages // pps),
            in_specs=[
                pl.BlockSpec((1, nrow, DIFF_HEAD_DIM), row_map),
                pl.BlockSpec((1, nrow, DIFF_HEAD_DIM), row_map),
                pl.BlockSpec((1, DIFF_HEADS, vdim), row_map),
            ] + k_specs + v_specs + [
                pl.BlockSpec((nrow, 1), const),
                pl.BlockSpec((4, DIFF_HEAD_DIM), const),
                pl.BlockSpec((1, vdim), const),
            ],
            out_specs=pl.BlockSpec((1, DIFF_HEADS, vdim), row_map),
            scratch_shapes=[pltpu.VMEM((nrow, DIFF_HEAD_DIM), BF16), pltpu.VMEM((nrow, vdim), F32),
                            pltpu.VMEM((nrow, 1), F32), pltpu.VMEM((nrow, 1), F32)],
        ),
        out_shape=jax.ShapeDtypeStruct((r, DIFF_HEADS, vdim), F32),
        compiler_params=_params("parallel", "arbitrary"),
        name="diff_decode",
    )(pt, q.reshape(r, nrow, DIFF_HEAD_DIM), k_new.reshape(r, nrow, DIFF_HEAD_DIM),
      v_new.reshape(r, DIFF_HEADS, vdim), *([cache_k] * pps), *([cache_v] * pps),
      slopes, lam_params, subln_g.reshape(1, vdim))
    return out.reshape(r, d)


def _expand_matrix(n_in, n_heads, width):
    r = jnp.arange(n_in)[:, None]
    c = jnp.arange(n_heads * width)[None, :]
    return (r == c // width).astype(BF16)


def _split3_dot(x, m):
    hi = x.astype(BF16)
    r1 = x - hi.astype(F32)
    mid = r1.astype(BF16)
    lo = (r1 - mid.astype(F32)).astype(BF16)
    return _dot(hi, m) + _dot(mid, m) + _dot(lo, m)


def _group_rms(y, seg, g, group):
    cols = []
    for c in range(y.shape[1] // group):
        yc = y[:, c * group:(c + 1) * group]
        ms = _split_dot(yc * yc, seg)
        cols.append(yc * lax.rsqrt(ms + NORM_EPS))
    return jnp.concatenate(cols, axis=1) * g


def _ssd_prompt_kernel(z_ref, x_ref, bc_ref, dt_ref, cw_ref, cb_ref, dtb_ref, alog_ref, dskip_ref, ng_ref,
                       tri_ref, e64_ref, e128_ref, seg_ref, y_ref, st_ref, carry_scr, yscr, *, chunk):
    c = pl.program_id(1)
    d_inner = x_ref.shape[1]
    gn = SSM_GROUPS * SSM_STATE

    @pl.when(c == 0)
    def _():
        carry_scr[...] = jnp.zeros_like(carry_scr)
        st_ref[...] = jnp.zeros_like(st_ref)

    def conv(raw, prev, w, b):
        ext = jnp.concatenate([prev, raw], axis=0)
        out = b + w[SSM_CONV - 1:SSM_CONV, :] * raw
        for k in range(1, SSM_CONV):
            shifted = pltpu.roll(ext, k, axis=0)[8:, :]
            out = out + w[SSM_CONV - 1 - k:SSM_CONV - k, :] * shifted
        return _silu(out)

    x_raw = x_ref[...]
    bc_raw = bc_ref[...]
    cw = cw_ref[...]
    cb = cb_ref[...]
    xs = conv(x_raw, carry_scr[:, :d_inner], cw[:, :d_inner], cb[:, :d_inner])
    bcm = conv(bc_raw, carry_scr[:, d_inner:], cw[:, d_inner:], cb[:, d_inner:])
    carry_scr[:, :d_inner] = x_raw[chunk - 8:, :]
    carry_scr[:, d_inner:] = bc_raw[chunk - 8:, :]
    bm = bcm[:, :gn].astype(BF16)
    cm = bcm[:, gn:].astype(BF16)

    dt = _softplus(dt_ref[...] + dtb_ref[...])
    a = dt * (-jnp.exp(alog_ref[...]))
    tri = tri_ref[...]
    acs = _split_dot_left(tri, a)
    acs_t = acs.T
    acs_col = _split3_dot(acs, e128_ref[...])
    acs_x = _split3_dot(acs, e64_ref[...])
    dt_x = _split3_dot(dt, e64_ref[...])
    xd = xs * dt_x
    last_x = acs_x[chunk - 1:chunk, :]
    xdw = (xd * jnp.exp(last_x - acs_x)).astype(BF16)
    xdb = xd.astype(BF16)
    ea_x = jnp.exp(acs_x)

    lrow = lax.broadcasted_iota(jnp.int32, (chunk, chunk), 0)
    scol = lax.broadcasted_iota(jnp.int32, (chunk, chunk), 1)
    tril = scol <= lrow
    lane = lax.broadcasted_iota(jnp.int32, (1, LANES), 1)
    first = lane < SSM_HEAD_DIM
    heads_per_group = SSM_HEADS // SSM_GROUPS

    for pair in range(SSM_HEADS // 2):
        g = (2 * pair) // heads_per_group
        bg = bm[:, g * SSM_STATE:(g + 1) * SSM_STATE]
        cg = cm[:, g * SSM_STATE:(g + 1) * SSM_STATE]
        cbm = _dot_nt(cg, bg)
        xd_pair = xdb[:, pair * LANES:(pair + 1) * LANES]
        ys = []
        for e in range(2):
            hd = 2 * pair + e
            seg = acs_col[:, hd * LANES:(hd + 1) * LANES] - acs_t[hd:hd + 1, :]
            decay = jnp.exp(jnp.where(tril, seg, -jnp.inf))
            ys.append(_dot((cbm * decay).astype(BF16), xd_pair))
        y_diag = jnp.where(first, ys[0], ys[1])
        st_pair = st_ref[2 * pair:2 * pair + 2].reshape(2 * SSM_HEAD_DIM, SSM_STATE)
        y_off = _dot_nt(cg, st_pair.astype(BF16)) * ea_x[:, pair * LANES:(pair + 1) * LANES]
        yscr[:, pair * LANES:(pair + 1) * LANES] = y_diag + y_off
        new = _dot_tn(xdw[:, pair * LANES:(pair + 1) * LANES], bg)
        for e in range(2):
            hd = 2 * pair + e
            dec = jnp.exp(acs_col[chunk - 1:chunk, hd * LANES:(hd + 1) * LANES])
            st_ref[hd] = st_ref[hd] * dec + new[e * SSM_HEAD_DIM:(e + 1) * SSM_HEAD_DIM, :]

    y = yscr[...] + xs * dskip_ref[...]
    y = y * _silu(z_ref[...])
    y_ref[...] = _group_rms(y, seg_ref[...], ng_ref[...], d_inner // SSM_GROUPS).astype(y_ref.dtype)


def _ssd_consts(d_inner):
    r = jnp.arange(SSM_CHUNK)
    tri = (r[:, None] >= r[None, :]).astype(BF16)
    return (tri, _expand_matrix(LANES, SSM_HEADS, SSM_HEAD_DIM), _expand_matrix(LANES, SSM_HEADS, LANES),
            _seg_mean_matrix(d_inner // SSM_GROUPS, d_inner // SSM_GROUPS))


def _pad_lanes(v):
    return jnp.pad(v, (0, LANES - v.shape[0])).reshape(1, LANES)


def _ssd_prompt(zx, conv_w, conv_b, dt_bias, a_log, d_skip, norm_g, batch, seq):
    d_inner = SSM_HEADS * SSM_HEAD_DIM
    chunk = SSM_CHUNK
    nc = seq // chunk
    conv_dim = conv_w.shape[1]
    tri, e64, e128, seg = _ssd_consts(d_inner)
    dt_blk = (2 * d_inner + 2 * SSM_GROUPS * SSM_STATE) // LANES
    const = lambda b, c: (0, 0)
    y, st = pl.pallas_call(
        functools.partial(_ssd_prompt_kernel, chunk=chunk),
        grid=(batch, nc),
        in_specs=[
            pl.BlockSpec((chunk, d_inner), lambda b, c: (b * nc + c, 0)),
            pl.BlockSpec((chunk, d_inner), lambda b, c: (b * nc + c, 1)),
            pl.BlockSpec((chunk, d_inner), lambda b, c: (b * nc + c, 2)),
            pl.BlockSpec((chunk, LANES), lambda b, c: (b * nc + c, dt_blk)),
            pl.BlockSpec((SSM_CONV, conv_dim), const),
            pl.BlockSpec((1, conv_dim), const),
            pl.BlockSpec((1, LANES), const),
            pl.BlockSpec((1, LANES), const),
            pl.BlockSpec((1, d_inner), const),
            pl.BlockSpec((1, d_inner), const),
            pl.BlockSpec(tri.shape, const),
            pl.BlockSpec(e64.shape, const),
            pl.BlockSpec(e128.shape, const),
            pl.BlockSpec(seg.shape, const),
        ],
        out_specs=[
            pl.BlockSpec((chunk, d_inner), lambda b, c: (b * nc + c, 0)),
            pl.BlockSpec((None, SSM_HEADS, SSM_HEAD_DIM, SSM_STATE), lambda b, c: (b, 0, 0, 0)),
        ],
        out_shape=[jax.ShapeDtypeStruct((batch * seq, d_inner), BF16),
                   jax.ShapeDtypeStruct((batch, SSM_HEADS, SSM_HEAD_DIM, SSM_STATE), F32)],
        scratch_shapes=[pltpu.VMEM((8, conv_dim), F32), pltpu.VMEM((chunk, d_inner), F32)],
        compiler_params=_params("parallel", "arbitrary"),
        name="ssd_prompt",
    )(zx, zx, zx, zx, conv_w, conv_b.reshape(1, -1), _pad_lanes(dt_bias), _pad_lanes(a_log),
      jnp.repeat(d_skip, SSM_HEAD_DIM).reshape(1, d_inner), norm_g.reshape(1, d_inner), tri, e64, e128, seg)
    return y, st


def _ssd_decode_kernel(z_ref, x_ref, bc_ref, dt_ref, cs_ref, st_ref, cw_ref, cb_ref, dtb_ref, alog_ref,
                       dskip_ref, ng_ref, e64_ref, e128_ref, seg_ref, y_ref, so_ref, *, rb):
    d_inner = x_ref.shape[1]
    gn = SSM_GROUPS * SSM_STATE
    cw = cw_ref[...]
    xbc_raw = jnp.concatenate([x_ref[...], bc_ref[...]], axis=1)
    out = cb_ref[...] + cw[SSM_CONV - 1:SSM_CONV, :] * xbc_raw
    for j in range(SSM_CONV - 1):
        out = out + cw[j:j + 1, :] * cs_ref[j]
    xbc = _silu(out)
    xs = xbc[:, :d_inner]
    bm = xbc[:, d_inner:d_inner + gn]
    cm = xbc[:, d_inner + gn:]

    dt = _softplus(dt_ref[...] + dtb_ref[...])
    a = dt * (-jnp.exp(alog_ref[...]))
    a_col = _split3_dot(a, e128_ref[...])
    dt_x = _split3_dot(dt, e64_ref[...])
    xd = (xs * dt_x).astype(BF16)
    bmb = bm.astype(BF16)
    cmb = cm.astype(BF16)
    rows = lax.broadcasted_iota(jnp.int32, (rb, 1), 0)
    hpg = SSM_HEADS // SSM_GROUPS
    gw = hpg * SSM_HEAD_DIM

    ycols = []
    for g in range(SSM_GROUPS):
        bg = bmb[:, g * SSM_STATE:(g + 1) * SSM_STATE]
        cg = cmb[:, g * SSM_STATE:(g + 1) * SSM_STATE]
        xg = xd[:, g * gw:(g + 1) * gw]
        yg = jnp.zeros((rb, gw), F32)
        for r in range(rb):
            outer = _dot_tn(jnp.where(rows == r, xg, jnp.zeros_like(xg)), bg)
            news = []
            for e in range(hpg):
                hd = g * hpg + e
                dec = jnp.exp(a_col[r:r + 1, hd * LANES:(hd + 1) * LANES])
                new = st_ref[r, hd] * dec + outer[e * SSM_HEAD_DIM:(e + 1) * SSM_HEAD_DIM, :]
                so_ref[r, hd] = new
                news.append(new)
            yr = _dot_nt(cg, jnp.concatenate(news, axis=0).astype(BF16))
            yg = jnp.where(rows == r, yr, yg)
        ycols.append(yg)
    y = jnp.concatenate(ycols, axis=1) + xs * dskip_ref[...]
    y = y * _silu(z_ref[...])
    y_ref[...] = _group_rms(y, seg_ref[...], ng_ref[...], d_inner // SSM_GROUPS).astype(y_ref.dtype)


def _ssd_decode(zx, conv_state, ssm_state, conv_w, conv_b, dt_bias, a_log, d_skip, norm_g):
    r = zx.shape[0]
    d_inner = SSM_HEADS * SSM_HEAD_DIM
    rb = 8
    conv_dim = conv_w.shape[1]
    _, e64, e128, seg = _ssd_consts(d_inner)
    dt_blk = (2 * d_inner + 2 * SSM_GROUPS * SSM_STATE) // LANES
    cs = jnp.transpose(conv_state, (1, 0, 2))
    const = lambda i: (0, 0)
    st_spec = pl.BlockSpec((rb, SSM_HEADS, SSM_HEAD_DIM, SSM_STATE), lambda i: (i, 0, 0, 0))
    y, st = pl.pallas_call(
        functools.partial(_ssd_decode_kernel, rb=rb),
        grid=(r // rb,),
        in_specs=[
            pl.BlockSpec((rb, d_inner), lambda i: (i, 0)),
            pl.BlockSpec((rb, d_inner), lambda i: (i, 1)),
            pl.BlockSpec((rb, d_inner), lambda i: (i, 2)),
            pl.BlockSpec((rb, LANES), lambda i: (i, dt_blk)),
            pl.BlockSpec((SSM_CONV - 1, rb, conv_dim), lambda i: (0, i, 0)),
            st_spec,
            pl.BlockSpec((SSM_CONV, conv_dim), const),
            pl.BlockSpec((1, conv_dim), const),
            pl.BlockSpec((1, LANES), const),
            pl.BlockSpec((1, LANES), const),
            pl.BlockSpec((1, d_inner), const),
            pl.BlockSpec((1, d_inner), const),
            pl.BlockSpec(e64.shape, const),
            pl.BlockSpec(e128.shape, const),
            pl.BlockSpec(seg.shape, const),
        ],
        out_specs=[pl.BlockSpec((rb, d_inner), lambda i: (i, 0)), st_spec],
        out_shape=[jax.ShapeDtypeStruct((r, d_inner), BF16),
                   jax.ShapeDtypeStruct(ssm_state.shape, F32)],
        compiler_params=_params("parallel"),
        name="ssd_decode",
    )(zx, zx, zx, zx, cs, ssm_state, conv_w, conv_b.reshape(1, -1), _pad_lanes(dt_bias), _pad_lanes(a_log),
      jnp.repeat(d_skip, SSM_HEAD_DIM).reshape(1, d_inner), norm_g.reshape(1, d_inner), e64, e128, seg)
    return y, st


def kernel(x_prompt, x_sample, cache_sb_k, cache_sb_v, cache_diff_k, cache_diff_v, state_ssm_conv, state_ssm,
           page_table, ffn_norm, ffn_w_gate, ffn_w_up, ffn_w_down, mix_norm, sb_w_qkv, sb_w_o,
           diff_w_qkv, diff_q_norm, diff_k_norm, diff_lambda_q1, diff_lambda_k1, diff_lambda_q2, diff_lambda_k2,
           diff_subln, diff_w_o, ssm_w_in, ssm_conv_w, ssm_conv_b, ssm_dt_bias, ssm_a_log, ssm_d, ssm_norm,
           ssm_w_out):
    bp, seq, d = x_prompt.shape
    bs = x_sample.shape[0]
    depth = ffn_norm.shape[0]
    xp = x_prompt.reshape(bp * seq, d)
    xs = x_sample.reshape(bs, d)
    wg, wu, wd = ffn_w_gate.astype(BF16), ffn_w_up.astype(BF16), ffn_w_down.astype(BF16)

    sb_kp, sb_vp, sb_ks, sb_vs = [], [], [], []
    d_kp, d_vp, d_ks, d_vs = [], [], [], []
    cv_p, ss_p, cv_s, ss_s = [], [], [], []
    for i in range(depth):
        kind, j = i % N_MIXERS, i // N_MIXERS
        xp = _ffn(xp, ffn_norm[i, 0], wg[i, 0], wu[i, 0], wd[i, 0])
        xs = _ffn(xs, ffn_norm[i, 0], wg[i, 0], wu[i, 0], wd[i, 0])
        if kind == 0:
            w = sb_w_qkv[j].astype(BF16)
            wq, wk, wv = w[:, :d], w[:, d:2 * d], w[:, 2 * d:]
            wo = sb_w_o[j].astype(BF16)
            q, k, v = _qkv(xp, mix_norm[i], wq, wk, wv)
            sb_kp.append(k.reshape(bp, seq, SB_HEADS, SB_HEAD_DIM))
            sb_vp.append(v.reshape(bp, seq, SB_HEADS, SB_HEAD_DIM))
            xp = _oproj(_sb_prompt(q, k, v, bp, seq), wo, xp)
            q, k, v = _qkv(xs, mix_norm[i], wq, wk, wv)
            sb_ks.append(k.reshape(bs, 1, SB_HEADS, SB_HEAD_DIM))
            sb_vs.append(v.reshape(bs, 1, SB_HEADS, SB_HEAD_DIM))
            xs = _oproj(_sb_decode(q, cache_sb_k, cache_sb_v, j, page_table), wo, xs)
        elif kind == 1:
            lam_init = 0.8 - 0.6 * math.exp(-0.3 * i)
            w = diff_w_qkv[j].astype(BF16)
            wq, wk, wv = w[:, :d], w[:, d:2 * d], w[:, 2 * d:]
            wo = diff_w_o[j].astype(BF16)
            qg = jnp.tile(diff_q_norm[j], d // DIFF_HEAD_DIM)
            kg = jnp.tile(diff_k_norm[j], d // DIFF_HEAD_DIM)
            lam_params = jnp.stack([diff_lambda_q1[j], diff_lambda_k1[j], diff_lambda_q2[j], diff_lambda_k2[j]])
            q, k, v = _qkv(xp, mix_norm[i], wq, wk, wv, qg, kg)
            d_kp.append(k.reshape(bp, seq, 2, DIFF_HEADS, DIFF_HEAD_DIM))
            d_vp.append(v.reshape(bp, seq, DIFF_HEADS, 2 * DIFF_HEAD_DIM))
            o = _diff_prompt(q, k, v, lam_params, diff_subln[j], lam_init, bp, seq)
            xp = _oproj(o, wo, xp)
            q, k, v = _qkv(xs, mix_norm[i], wq, wk, wv, qg, kg)
            d_ks.append(k.reshape(bs, 1, 2, DIFF_HEADS, DIFF_HEAD_DIM))
            d_vs.append(v.reshape(bs, 1, DIFF_HEADS, 2 * DIFF_HEAD_DIM))
            o = _diff_decode(q, k, v, cache_diff_k, cache_diff_v, j, page_table, lam_params, diff_subln[j],
                             lam_init)
            xs = _oproj(o, wo, xs)
        else:
            d_inner = SSM_HEADS * SSM_HEAD_DIM
            conv_dim = ssm_conv_w.shape[2]
            w_in = jnp.pad(ssm_w_in[j], ((0, 0), (0, LANES - SSM_HEADS))).astype(BF16)
            wo = ssm_w_out[j].astype(BF16)
            sp = (ssm_conv_w[j], ssm_conv_b[j], ssm_dt_bias[j], ssm_a_log[j], ssm_d[j], ssm_norm[j])
            tn = _tile(w_in.shape[1], 896)
            zx = _norm_matmul(xp, mix_norm[i], w_in, tn)
            y, st = _ssd_prompt(zx, *sp, bp, seq)
            cv_p.append(zx.reshape(bp, seq, -1)[:, seq - (SSM_CONV - 1):, d_inner:d_inner + conv_dim])
            ss_p.append(st)
            xp = _oproj(y, wo, xp)
            zx = _norm_matmul(xs, mix_norm[i], w_in, tn)
            y, st = _ssd_decode(zx, state_ssm_conv[j], state_ssm[j], *sp)
            cv_s.append(jnp.concatenate([state_ssm_conv[j][:, 1:], zx[:, None, d_inner:d_inner + conv_dim]], axis=1))
            ss_s.append(st)
            xs = _oproj(y, wo, xs)
        xp = _ffn(xp, ffn_norm[i, 1], wg[i, 1], wu[i, 1], wd[i, 1])
        xs = _ffn(xs, ffn_norm[i, 1], wg[i, 1], wu[i, 1], wd[i, 1])
    return (xp.reshape(bp, seq, d), xs.reshape(bs, 1, d),
            jnp.stack(sb_kp), jnp.stack(sb_vp), jnp.stack(sb_ks), jnp.stack(sb_vs),
            jnp.stack(d_kp), jnp.stack(d_vp), jnp.stack(d_ks), jnp.stack(d_vs),
            jnp.stack(cv_p), jnp.stack(ss_p), jnp.stack(cv_s), jnp.stack(ss_s))
```

```python
import functools
import math

import jax
import jax.numpy as jnp
from jax import lax
from jax.experimental import pallas as pl
from jax.experimental.pallas import tpu as pltpu

F32 = jnp.float32
BF16 = jnp.bfloat16

NORM_EPS = 1e-6
N_MIXERS = 3
SB_HEADS = 16
SB_HEAD_DIM = 64
DIFF_HEADS = 8
DIFF_HEAD_DIM = 64
SSM_HEAD_DIM = 64
SSM_HEADS = 32
SSM_GROUPS = 8
SSM_STATE = 128
SSM_CONV = 4
SSM_CHUNK = 128
LANES = 128
VMEM_LIMIT = 48 * 1024 * 1024


def _params(*sem):
    return pltpu.CompilerParams(dimension_semantics=sem, vmem_limit_bytes=VMEM_LIMIT)


def _tile(n, pref):
    t = min(n, pref)
    while n % t:
        t //= 2
    return t


def _dot(a, b):
    return jnp.dot(a, b, preferred_element_type=F32)


def _dot_nt(a, b):
    return lax.dot_general(a, b, (((1,), (1,)), ((), ())), preferred_element_type=F32)


def _dot_tn(a, b):
    return lax.dot_general(a, b, (((0,), (0,)), ((), ())), preferred_element_type=F32)


def _split_dot(x, m):
    hi = x.astype(BF16)
    lo = (x - hi.astype(F32)).astype(BF16)
    return _dot(hi, m) + _dot(lo, m)


def _split_dot_left(m, x):
    hi = x.astype(BF16)
    lo = (x - hi.astype(F32)).astype(BF16)
    return _dot(m, hi) + _dot(m, lo)


def _rms_rows(x, g):
    return x * lax.rsqrt(jnp.mean(x * x, axis=-1, keepdims=True) + NORM_EPS) * g


def _silu(x):
    return x / (1.0 + jnp.exp(-x))


def _softplus(z):
    return jnp.maximum(z, 0.0) + jnp.log(1.0 + jnp.exp(-jnp.abs(z)))


def _ffn_kernel(x_ref, g_ref, wg_ref, wu_ref, wd_ref, o_ref, h_scr, acc_scr):
    f = pl.program_id(1)

    @pl.when(f == 0)
    def _():
        h_scr[...] = _rms_rows(x_ref[...], g_ref[...]).astype(BF16)
        acc_scr[...] = jnp.zeros_like(acc_scr)

    h = h_scr[...]
    gate = _dot(h, wg_ref[...])
    up = _dot(h, wu_ref[...])
    act = (_silu(gate) * up).astype(BF16)
    acc_scr[...] += _dot(act, wd_ref[...])

    @pl.when(f == pl.num_programs(1) - 1)
    def _():
        o_ref[...] = x_ref[...] + 0.5 * acc_scr[...]


def _ffn(x, g, wg, wu, wd):
    m, d = x.shape
    dff = wg.shape[1]
    tm = _tile(m, 1024)
    tf = _tile(dff, 256)
    return pl.pallas_call(
        _ffn_kernel,
        grid=(m // tm, dff // tf),
        in_specs=[
            pl.BlockSpec((tm, d), lambda i, f: (i, 0)),
            pl.BlockSpec((1, d), lambda i, f: (0, 0)),
            pl.BlockSpec((d, tf), lambda i, f: (0, f)),
            pl.BlockSpec((d, tf), lambda i, f: (0, f)),
            pl.BlockSpec((tf, d), lambda i, f: (f, 0)),
        ],
        out_specs=pl.BlockSpec((tm, d), lambda i, f: (i, 0)),
        out_shape=jax.ShapeDtypeStruct((m, d), F32),
        scratch_shapes=[pltpu.VMEM((tm, d), BF16), pltpu.VMEM((tm, d), F32)],
        compiler_params=_params("parallel", "arbitrary"),
        name="ffn",
    )(x, g.reshape(1, d), wg, wu, wd)


def _norm_matmul_kernel(x_ref, g_ref, w_ref, o_ref, h_scr):
    @pl.when(pl.program_id(1) == 0)
    def _():
        h_scr[...] = _rms_rows(x_ref[...], g_ref[...]).astype(BF16)

    o_ref[...] = _dot(h_scr[...], w_ref[...])


def _norm_matmul(x, g, w, tn):
    m, d = x.shape
    n = w.shape[1]
    tm = _tile(m, 1024)
    return pl.pallas_call(
        _norm_matmul_kernel,
        grid=(m // tm, n // tn),
        in_specs=[
            pl.BlockSpec((tm, d), lambda i, j: (i, 0)),
            pl.BlockSpec((1, d), lambda i, j: (0, 0)),
            pl.BlockSpec((d, tn), lambda i, j: (0, j)),
        ],
        out_specs=pl.BlockSpec((tm, tn), lambda i, j: (i, j)),
        out_shape=jax.ShapeDtypeStruct((m, n), F32),
        scratch_shapes=[pltpu.VMEM((tm, d), BF16)],
        compiler_params=_params("parallel", "arbitrary"),
        name="norm_matmul",
    )(x, g.reshape(1, d), w)


def _head_rms(x, seg, g):
    cols = []
    for c in range(x.shape[1] // LANES):
        xc = x[:, c * LANES:(c + 1) * LANES]
        ms = _split_dot(xc * xc, seg)
        cols.append(xc * lax.rsqrt(ms + NORM_EPS))
    return jnp.concatenate(cols, axis=1) * g


def _qkv_kernel(x_ref, g_ref, wq_ref, wk_ref, wv_ref, seg_ref, qg_ref, kg_ref,
                q_ref, k_ref, v_ref, *, qk_norm):
    h = _rms_rows(x_ref[...], g_ref[...]).astype(BF16)
    q = _dot(h, wq_ref[...])
    k = _dot(h, wk_ref[...])
    if qk_norm:
        q = _head_rms(q, seg_ref[...], qg_ref[...])
        k = _head_rms(k, seg_ref[...], kg_ref[...])
    q_ref[...] = q
    k_ref[...] = k
    v_ref[...] = _dot(h, wv_ref[...])


def _seg_mean_matrix(n, group):
    r = jnp.arange(n)
    return jnp.where((r[:, None] // group) == (r[None, :] // group), 1.0 / group, 0.0).astype(BF16)


def _qkv(x, g, wq, wk, wv, qg=None, kg=None):
    m, d = x.shape
    tm = _tile(m, 512)
    qk_norm = qg is not None
    if not qk_norm:
        qg = kg = jnp.ones((d,), F32)
    row = lambda i: (i, 0)
    const = lambda i: (0, 0)
    out = jax.ShapeDtypeStruct((m, d), F32)
    return pl.pallas_call(
        functools.partial(_qkv_kernel, qk_norm=qk_norm),
        grid=(m // tm,),
        in_specs=[
            pl.BlockSpec((tm, d), row),
            pl.BlockSpec((1, d), const),
            pl.BlockSpec((d, d), const),
            pl.BlockSpec((d, d), const),
            pl.BlockSpec((d, d), const),
            pl.BlockSpec((LANES, LANES), const),
            pl.BlockSpec((1, d), const),
            pl.BlockSpec((1, d), const),
        ],
        out_specs=[pl.BlockSpec((tm, d), row)] * 3,
        out_shape=[out, out, out],
        compiler_params=_params("parallel"),
        name="qkv_norm" if qk_norm else "qkv",
    )(x, g.reshape(1, d), wq, wk, wv, _seg_mean_matrix(LANES, DIFF_HEAD_DIM),
      qg.reshape(1, d), kg.reshape(1, d))


def _oproj_kernel(o_ref, w_ref, r_ref, y_ref):
    y_ref[...] = r_ref[...] + _dot(o_ref[...].astype(BF16), w_ref[...])


def _oproj(o, w, res):
    m, k = o.shape
    d = w.shape[1]
    tm = _tile(m, 512)
    return pl.pallas_call(
        _oproj_kernel,
        grid=(m // tm,),
        in_specs=[
            pl.BlockSpec((tm, k), lambda i: (i, 0)),
            pl.BlockSpec((k, d), lambda i: (0, 0)),
            pl.BlockSpec((tm, d), lambda i: (i, 0)),
        ],
        out_specs=pl.BlockSpec((tm, d), lambda i: (i, 0)),
        out_shape=jax.ShapeDtypeStruct((m, d), F32),
        compiler_params=_params("parallel"),
        name="oproj",
    )(o, w, res)


def _suffix_matrix(n):
    r = jnp.arange(n)
    return (r[:, None] > r[None, :]).astype(BF16)


def _sb_prompt_kernel(q_ref, k_ref, v_ref, u_ref, o_ref, kb_scr, vb_scr, acc_scr, *, tk, halves):
    qi = pl.program_id(2)

    @pl.when(qi == 0)
    def _():
        kb_scr[...] = k_ref[...].astype(BF16)
        vb_scr[...] = v_ref[...].astype(BF16)

    q2 = q_ref[...] * (SB_HEAD_DIM ** -0.5)
    lane = lax.broadcasted_iota(jnp.int32, (1, LANES), 1)
    row = lax.broadcasted_iota(jnp.int32, (tk, tk), 0)
    col = lax.broadcasted_iota(jnp.int32, (tk, tk), 1)
    causal = col < row
    u = u_ref[...]
    base = qi * halves
    chains = [(hh, a) for hh in range(2) for a in range(halves)]
    qms = []
    for hh, a in chains:
        in_head = (lane // SB_HEAD_DIM) == hh
        qms.append(jnp.where(in_head, q2[a * tk:(a + 1) * tk], 0.0).astype(BF16))

    def tiles(cs, kjs, rs, masked):
        starts = [pl.multiple_of(kj * tk, tk) for kj in kjs]
        zs = [_dot_nt(qms[c], kb_scr[pl.ds(st, tk), :]) for c, st in zip(cs, starts)]
        sps = [_softplus(z) for z in zs]
        if masked:
            sps = [jnp.where(causal, sp, 0.0) for sp in sps]
        cums = [_dot(sp.astype(BF16), u) for sp in sps]
        ws = [jnp.exp(z - sp - cum - r) for z, sp, cum, r in zip(zs, sps, cums, rs)]
        if masked:
            ws = [jnp.where(causal, w, 0.0) for w in ws]
        for c, st, w in zip(cs, starts, ws):
            pv = _dot(w.astype(BF16), vb_scr[pl.ds(st, tk), :])
            if masked:
                acc_scr[c] = pv
            else:
                acc_scr[c] += pv
        return [r + cum[:, :1] + sp[:, :1] for r, cum, sp in zip(rs, cums, sps)]

    n_chain = len(chains)
    rs = [jnp.zeros((tk, 1), F32)] * n_chain
    for t in range(halves):
        cs = [c for c in range(n_chain) if chains[c][1] >= t]
        new = tiles(cs, [base + chains[c][1] - t for c in cs], [rs[c] for c in cs], t == 0)
        for c, r in zip(cs, new):
            rs[c] = r

    depth = 2 if halves % 2 == 0 else 1

    def body(j, rs):
        kjs = [base - 1 - depth * j - t for t in range(depth)]
        starts = [pl.multiple_of(kj * tk, tk) for kj in kjs]
        zs = [[_dot_nt(qms[c], kb_scr[pl.ds(st, tk), :]) for c in range(n_chain)] for st in starts]
        sps = [[_softplus(z) for z in zt] for zt in zs]
        cums = [[_dot(sp.astype(BF16), u) for sp in spt] for spt in sps]
        rs = list(rs)
        for c in range(n_chain):
            pv = None
            for t in range(depth):
                w = jnp.exp(zs[t][c] - sps[t][c] - cums[t][c] - rs[c]).astype(BF16)
                d = _dot(w, vb_scr[pl.ds(starts[t], tk), :])
                pv = d if pv is None else pv + d
                rs[c] = rs[c] + cums[t][c][:, :1] + sps[t][c][:, :1]
            acc_scr[c] += pv
        return tuple(rs)

    lax.fori_loop(0, base // depth, body, tuple(rs))
    for a in range(halves):
        o_ref[a * tk:(a + 1) * tk, :] = jnp.where(
            lane < SB_HEAD_DIM, acc_scr[a], acc_scr[halves + a]).astype(o_ref.dtype)


ATTN_TK = 256
ATTN_HALVES = 2


def _attn_tiles(seq):
    tk = _tile(seq, ATTN_TK)
    halves = ATTN_HALVES if seq % (ATTN_HALVES * tk) == 0 else 1
    return tk, halves


def _sb_prompt(q, k, v, batch, seq):
    m, d = q.shape
    tk, halves = _attn_tiles(seq)
    tq = tk * halves
    nq = seq // tq
    pairs = d // LANES
    k3 = k.reshape(batch, seq, d)
    v3 = v.reshape(batch, seq, d)
    return pl.pallas_call(
        functools.partial(_sb_prompt_kernel, tk=tk, halves=halves),
        grid=(batch, pairs, nq),
        in_specs=[
            pl.BlockSpec((tq, LANES), lambda b, p, i: (b * nq + i, p)),
            pl.BlockSpec((None, seq, LANES), lambda b, p, i: (b, 0, p)),
            pl.BlockSpec((None, seq, LANES), lambda b, p, i: (b, 0, p)),
            pl.BlockSpec((tk, tk), lambda b, p, i: (0, 0)),
        ],
        out_specs=pl.BlockSpec((tq, LANES), lambda b, p, i: (b * nq + i, p)),
        out_shape=jax.ShapeDtypeStruct((m, d), BF16),
        scratch_shapes=[pltpu.VMEM((seq, LANES), BF16), pltpu.VMEM((seq, LANES), BF16),
                        pltpu.VMEM((2 * halves, tk, LANES), F32)],
        compiler_params=_params("parallel", "parallel", "arbitrary"),
        name="sb_prompt",
    )(q, k3, v3, _suffix_matrix(tk))


DECODE_PAGES_PER_STEP = 4


def _pages_per_step(n_pages):
    pps = DECODE_PAGES_PER_STEP
    while n_pages % pps:
        pps //= 2
    return pps


def _slot_minor(cache):
    nd = cache.ndim
    t = jnp.transpose(cache, (0, 1) + tuple(range(3, nd)) + (2,))
    return t.reshape(cache.shape[0], cache.shape[1], -1, cache.shape[2])


def _head_rows(x_row, n_rows):
    width = x_row.shape[1] // n_rows
    r = lax.broadcasted_iota(jnp.int32, (n_rows, n_rows * width), 0)
    c = lax.broadcasted_iota(jnp.int32, (n_rows, n_rows * width), 1)
    own = (c // width) == r
    return jnp.where(own, x_row, 0.0), own


def _sb_decode_kernel(pt_ref, q_ref, *refs, pps):
    k_refs, v_refs = refs[:pps], refs[pps:2 * pps]
    u_ref, o_ref, q_scr, acc_scr, r_scr = refs[2 * pps:]
    p = pl.program_id(1)

    @pl.when(p == 0)
    def _():
        qrows, _ = _head_rows(q_ref[0] * (SB_HEAD_DIM ** -0.5), SB_HEADS)
        q_scr[...] = qrows.astype(BF16)
        acc_scr[...] = jnp.zeros_like(acc_scr)
        r_scr[...] = jnp.zeros_like(r_scr)

    q = q_scr[...]
    r = r_scr[...]
    acc = acc_scr[...]
    zts = [_dot(q, k_ref[...].astype(BF16)) for k_ref in k_refs]
    sps = [_softplus(zt) for zt in zts]
    cums = [_split_dot(sp, u_ref[...]) for sp in sps]
    for zt, sp, cum, v_ref in zip(zts, sps, cums, v_refs):
        wb = jnp.exp(zt - sp - cum - r).astype(BF16)
        acc = acc + _dot_nt(wb, v_ref[...].astype(BF16))
        r = r + cum[:, :1] + sp[:, :1]
    acc_scr[...] = acc
    r_scr[...] = r

    @pl.when(p == pl.num_programs(1) - 1)
    def _():
        _, own = _head_rows(jnp.zeros((1, acc.shape[1]), F32), SB_HEADS)
        o_ref[0] = jnp.sum(jnp.where(own, acc, 0.0), axis=0, keepdims=True)


def _sb_decode(q, cache_k, cache_v, layer, page_table):
    r, d = q.shape
    n_pages = page_table.shape[1]
    page = cache_k.shape[2]
    pps = _pages_per_step(n_pages)
    pt = page_table.reshape(-1)

    cache_k, cache_v = _slot_minor(cache_k), _slot_minor(cache_v)

    def kv_map(i):
        return lambda b, p, pt_ref: (layer, pt_ref[b * n_pages + n_pages - 1 - (p * pps + i)], 0, 0)

    kv_specs = [pl.BlockSpec((None, None, d, page), kv_map(i)) for i in range(pps)]
    row_map = lambda b, p, pt_ref: (b, 0, 0)
    out = pl.pallas_call(
        functools.partial(_sb_decode_kernel, pps=pps),
        grid_spec=pltpu.PrefetchScalarGridSpec(
            num_scalar_prefetch=1,
            grid=(r, n_pages // pps),
            in_specs=[pl.BlockSpec((1, 1, d), row_map)] + kv_specs + kv_specs
            + [pl.BlockSpec((page, page), lambda b, p, pt_ref: (0, 0))],
            out_specs=pl.BlockSpec((1, 1, d), row_map),
            scratch_shapes=[pltpu.VMEM((SB_HEADS, d), BF16), pltpu.VMEM((SB_HEADS, d), F32),
                            pltpu.VMEM((SB_HEADS, 1), F32)],
        ),
        out_shape=jax.ShapeDtypeStruct((r, 1, d), F32),
        compiler_params=_params("parallel", "arbitrary"),
        name="sb_decode",
    )(pt, q.reshape(r, 1, d), *([cache_k] * pps), *([cache_v] * pps), _suffix_matrix(page))
    return out.reshape(r, d)


def _lambda(lq1, lk1, lq2, lk2, lam_init):
    return (jnp.exp(jnp.sum(lq1 * lk1, axis=-1, keepdims=True))
            - jnp.exp(jnp.sum(lq2 * lk2, axis=-1, keepdims=True)) + lam_init)


def _diff_prompt_kernel(slope_ref, q0_ref, q1_ref, k0_ref, k1_ref, v_ref, lam_ref, sg_ref, o_ref,
                        k0_scr, k1_scr, vb_scr, acc_scr, *, tk, halves, lam_init):
    h = pl.program_id(1)
    qi = pl.program_id(2)

    @pl.when(qi == 0)
    def _():
        k0_scr[...] = k0_ref[...].astype(BF16)
        k1_scr[...] = k1_ref[...].astype(BF16)
        vb_scr[...] = v_ref[...].astype(BF16)

    slope = slope_ref[h]
    lane = lax.broadcasted_iota(jnp.int32, (1, LANES), 1)
    in_head = (lane // DIFF_HEAD_DIM) == (h % 2)
    row = lax.broadcasted_iota(jnp.int32, (tk, tk), 0)
    col = lax.broadcasted_iota(jnp.int32, (tk, tk), 1)
    causal = col <= row
    rel = lax.broadcasted_iota(jnp.int32, (1, tk), 1).astype(F32)
    base = qi * halves
    k_scrs = (k0_scr, k1_scr)
    chains = [(c, a) for c in range(2) for a in range(halves)]
    qms = []
    for c, a in chains:
        q = (q0_ref, q1_ref)[c][a * tk:(a + 1) * tk, :]
        qms.append(jnp.where(in_head, q * (DIFF_HEAD_DIM ** -0.5), 0.0).astype(BF16))

    def tiles(cs, kjs, states, diag):
        starts = [pl.multiple_of(kj * tk, tk) for kj in kjs]
        ss = [_dot_nt(qms[ci], k_scrs[chains[ci][0]][pl.ds(st, tk), :])
              + slope * (rel + ((kj - base) * tk).astype(F32)) for ci, st, kj in zip(cs, starts, kjs)]
        if diag:
            ss = [jnp.where(causal, s, -jnp.inf) for s in ss]
            ms = [jnp.max(s, axis=-1, keepdims=True) for s in ss]
        else:
            ms = [jnp.maximum(m, jnp.max(s, axis=-1, keepdims=True)) for (m, _), s in zip(states, ss)]
        ps = [jnp.exp(s - m) for s, m in zip(ss, ms)]
        out = []
        for i, (ci, st, p) in enumerate(zip(cs, starts, ps)):
            pv = _dot(p.astype(BF16), vb_scr[pl.ds(st, tk), :])
            psum = jnp.sum(p, axis=-1, keepdims=True)
            if diag:
                acc_scr[ci] = pv
                out.append((ms[i], psum))
            else:
                alpha = jnp.exp(states[i][0] - ms[i])
                acc_scr[ci] = alpha * acc_scr[ci] + pv
                out.append((ms[i], alpha * states[i][1] + psum))
        return out

    n_chain = len(chains)
    states = [None] * n_chain
    for t in range(halves):
        cs = [ci for ci in range(n_chain) if chains[ci][1] >= t]
        new = tiles(cs, [base + chains[ci][1] - t for ci in cs], [states[ci] for ci in cs], t == 0)
        for ci, st in zip(cs, new):
            states[ci] = st

    def body(j, states):
        return tuple(tiles(list(range(n_chain)), [j] * n_chain, list(states), False))

    states = lax.fori_loop(0, base, body, tuple(states))
    lam = _lambda(lam_ref[0:1, :], lam_ref[1:2, :], lam_ref[2:3, :], lam_ref[3:4, :], lam_init)
    for a in range(halves):
        o = acc_scr[a] / states[a][1] - lam * (acc_scr[halves + a] / states[halves + a][1])
        o = o * lax.rsqrt(jnp.mean(o * o, axis=-1, keepdims=True) + NORM_EPS) * sg_ref[...]
        o_ref[a * tk:(a + 1) * tk, :] = (o * (1.0 - lam_init)).astype(o_ref.dtype)


def _alibi_slopes(n_heads):
    return jnp.exp2(-8.0 * jnp.arange(1, n_heads + 1, dtype=F32) / n_heads)


def _diff_prompt(q, k, v, lam_params, subln_g, lam_init, batch, seq):
    m, d = q.shape
    tk, halves = _attn_tiles(seq)
    tq = tk * halves
    nq = seq // tq
    half = d // (2 * LANES)
    k3 = k.reshape(batch, seq, d)
    v3 = v.reshape(batch, seq, d)
    return pl.pallas_call(
        functools.partial(_diff_prompt_kernel, tk=tk, halves=halves, lam_init=lam_init),
        grid_spec=pltpu.PrefetchScalarGridSpec(
            num_scalar_prefetch=0,
            grid=(batch, DIFF_HEADS, nq),
            in_specs=[
                pl.BlockSpec(memory_space=pltpu.SMEM),
                pl.BlockSpec((tq, LANES), lambda b, h, i: (b * nq + i, h // 2)),
                pl.BlockSpec((tq, LANES), lambda b, h, i: (b * nq + i, half + h // 2)),
                pl.BlockSpec((None, seq, LANES), lambda b, h, i: (b, 0, h // 2)),
                pl.BlockSpec((None, seq, LANES), lambda b, h, i: (b, 0, half + h // 2)),
                pl.BlockSpec((None, seq, LANES), lambda b, h, i: (b, 0, h)),
                pl.BlockSpec((4, DIFF_HEAD_DIM), lambda b, h, i: (0, 0)),
                pl.BlockSpec((1, LANES), lambda b, h, i: (0, 0)),
            ],
            out_specs=pl.BlockSpec((tq, LANES), lambda b, h, i: (b * nq + i, h)),
            scratch_shapes=[pltpu.VMEM((seq, LANES), BF16)] * 3 + [pltpu.VMEM((2 * halves, tk, LANES), F32)],
        ),
        out_shape=jax.ShapeDtypeStruct((m, d), BF16),
        compiler_params=_params("parallel", "parallel", "arbitrary"),
        name="diff_prompt",
    )(_alibi_slopes(DIFF_HEADS), q, q, k3, k3, v3, lam_params, subln_g.reshape(1, LANES))


def _diff_decode_kernel(pt_ref, q_ref, kn_ref, vn_ref, *refs, pps, lam_init, past):
    k_refs, v_refs = refs[:pps], refs[pps:2 * pps]
    slope_ref, lam_ref, sg_ref, rep_ref, o_ref, q_scr, acc_scr, m_scr, l_scr = refs[2 * pps:]
    p = pl.program_id(1)
    nrow = 2 * DIFF_HEADS
    page = k_refs[0].shape[1]

    @pl.when(p == 0)
    def _():
        qb = (q_ref[0] * (DIFF_HEAD_DIM ** -0.5)).astype(BF16).astype(F32)
        qrows, _ = _head_rows(qb, nrow)
        q_scr[...] = qrows.astype(BF16)
        kn = kn_ref[0].astype(BF16).astype(F32)
        m_scr[...] = jnp.sum(qrows * kn, axis=-1, keepdims=True)
        l_scr[...] = jnp.ones_like(l_scr)
        vn = vn_ref[0].astype(BF16).astype(F32)
        acc_scr[...] = jnp.concatenate([vn, vn], axis=0)

    q = q_scr[...]
    big = (nrow, page * DIFF_HEADS)
    own_head = (lax.broadcasted_iota(jnp.int32, big, 1) % DIFF_HEADS
                == lax.broadcasted_iota(jnp.int32, big, 0) % DIFF_HEADS)
    m, l, acc = m_scr[...], l_scr[...], acc_scr[...]
    ss = []
    for i, k_ref in enumerate(k_refs):
        pos = (p * pps + i) * page + lax.broadcasted_iota(jnp.int32, (1, page), 1)
        ss.append(_dot(q, k_ref[...].astype(BF16)) - slope_ref[...] * (past - pos).astype(F32))
    m_new = m
    for s in ss:
        m_new = jnp.maximum(m_new, jnp.max(s, axis=-1, keepdims=True))
    alpha = jnp.exp(m - m_new)
    prs = [jnp.exp(s - m_new) for s in ss]
    l = alpha * l
    acc = alpha * acc
    for pr, v_ref in zip(prs, v_refs):
        l = l + jnp.sum(pr, axis=-1, keepdims=True)
        spread = _dot(pr.astype(BF16), rep_ref[...])
        pbig = jnp.where(own_head, spread, 0.0).astype(BF16)
        acc = acc + _dot(pbig, v_ref[...].astype(BF16))
    m = m_new
    m_scr[...] = m
    l_scr[...] = l
    acc_scr[...] = acc

    @pl.when(p == pl.num_programs(1) - 1)
    def _():
        a = acc / l
        lam = _lambda(lam_ref[0:1, :], lam_ref[1:2, :], lam_ref[2:3, :], lam_ref[3:4, :], lam_init)
        dd = a[:DIFF_HEADS] - lam * a[DIFF_HEADS:]
        dn = dd * lax.rsqrt(jnp.mean(dd * dd, axis=-1, keepdims=True) + NORM_EPS)
        o_ref[0] = dn * sg_ref[...] * (1.0 - lam_init)


def _diff_decode(q, k_new, v_new, cache_k, cache_v, layer, page_table, lam_params, subln_g, lam_init):
    r, d = q.shape
    n_pages = page_table.shape[1]
    page = cache_k.shape[2]
    pps = _pages_per_step(n_pages)
    nrow = 2 * DIFF_HEADS
    vdim = 2 * DIFF_HEAD_DIM
    pt = page_table.reshape(-1)

    def page_of(i):
        return lambda b, p, pt_ref: pt_ref[b * n_pages + p * pps + i]

    cache_k = _slot_minor(cache_k)
    cache_v = cache_v.reshape(cache_v.shape[:2] + (page * DIFF_HEADS, vdim))
    rep = _expand_matrix(page, page, DIFF_HEADS)
    k_specs = [pl.BlockSpec((None, None, d, page),
                            lambda b, p, pt_ref, f=page_of(i): (layer, f(b, p, pt_ref), 0, 0))
               for i in range(pps)]
    v_specs = [pl.BlockSpec((None, None, page * DIFF_HEADS, vdim),
                            lambda b, p, pt_ref, f=page_of(i): (layer, f(b, p, pt_ref), 0, 0))
               for i in range(pps)]
    row_map = lambda b, p, pt_ref: (b, 0, 0)
    const = lambda b, p, pt_ref: (0, 0)
    slopes = jnp.tile(_alibi_slopes(DIFF_HEADS), 2).reshape(nrow, 1)
    out = pl.pallas_call(
        functools.partial(_diff_decode_kernel, pps=pps, lam_init=lam_init, past=n_pages * page),
        grid_spec=pltpu.PrefetchScalarGridSpec(
            num_scalar_prefetch=1,
            grid=(r, n_pages // pps),
            in_specs=[
                pl.BlockSpec((1, 1, d), row_map),
                pl.BlockSpec((1, 1, d), row_map),
                pl.BlockSpec((1, DIFF_HEADS, vdim), row_map),
            ] + k_specs + v_specs + [
                pl.BlockSpec((nrow, 1), const),
                pl.BlockSpec((4, DIFF_HEAD_DIM), const),
                pl.BlockSpec((1, vdim), const),
                pl.BlockSpec(rep.shape, const),
            ],
            out_specs=pl.BlockSpec((1, DIFF_HEADS, vdim), row_map),
            scratch_shapes=[pltpu.VMEM((nrow, d), BF16), pltpu.VMEM((nrow, vdim), F32),
                            pltpu.VMEM((nrow, 1), F32), pltpu.VMEM((nrow, 1), F32)],
        ),
        out_shape=jax.ShapeDtypeStruct((r, DIFF_HEADS, vdim), F32),
        compiler_params=_params("parallel", "arbitrary"),
        name="diff_decode",
    )(pt, q.reshape(r, 1, d), k_new.reshape(r, 1, d), v_new.reshape(r, DIFF_HEADS, vdim),
      *([cache_k] * pps), *([cache_v] * pps), slopes, lam_params, subln_g.reshape(1, vdim), rep)
    return out.reshape(r, d)


def _expand_matrix(n_in, n_heads, width):
    r = jnp.arange(n_in)[:, None]
    c = jnp.arange(n_heads * width)[None, :]
    return (r == c // width).astype(BF16)


def _split3_dot(x, m):
    hi = x.astype(BF16)
    r1 = x - hi.astype(F32)
    mid = r1.astype(BF16)
    lo = (r1 - mid.astype(F32)).astype(BF16)
    return _dot(hi, m) + _dot(mid, m) + _dot(lo, m)


def _group_rms(y, seg, g, group):
    cols = []
    for c in range(y.shape[1] // group):
        yc = y[:, c * group:(c + 1) * group]
        ms = _split_dot(yc * yc, seg)
        cols.append(yc * lax.rsqrt(ms + NORM_EPS))
    return jnp.concatenate(cols, axis=1) * g


def _ssd_prompt_kernel(z_ref, x_ref, bc_ref, dt_ref, cw_ref, cb_ref, dtb_ref, alog_ref, dskip_ref, ng_ref,
                       tri_ref, e64_ref, e128_ref, seg_ref, y_ref, st_ref, carry_scr, yscr, *, chunk):
    c = pl.program_id(1)
    d_inner = x_ref.shape[1]
    gn = SSM_GROUPS * SSM_STATE

    @pl.when(c == 0)
    def _():
        carry_scr[...] = jnp.zeros_like(carry_scr)
        st_ref[...] = jnp.zeros_like(st_ref)

    def conv(raw, prev, w, b):
        ext = jnp.concatenate([prev, raw], axis=0)
        out = b + w[SSM_CONV - 1:SSM_CONV, :] * raw
        for k in range(1, SSM_CONV):
            shifted = pltpu.roll(ext, k, axis=0)[8:, :]
            out = out + w[SSM_CONV - 1 - k:SSM_CONV - k, :] * shifted
        return _silu(out)

    x_raw = x_ref[...]
    bc_raw = bc_ref[...]
    cw = cw_ref[...]
    cb = cb_ref[...]
    xs = conv(x_raw, carry_scr[:, :d_inner], cw[:, :d_inner], cb[:, :d_inner])
    bcm = conv(bc_raw, carry_scr[:, d_inner:], cw[:, d_inner:], cb[:, d_inner:])
    carry_scr[:, :d_inner] = x_raw[chunk - 8:, :]
    carry_scr[:, d_inner:] = bc_raw[chunk - 8:, :]
    bm = bcm[:, :gn].astype(BF16)
    cm = bcm[:, gn:].astype(BF16)

    dt = _softplus(dt_ref[...] + dtb_ref[...])
    a = dt * (-jnp.exp(alog_ref[...]))
    tri = tri_ref[...]
    acs = _split_dot_left(tri, a)
    acs_t = acs.T
    acs_col = _split3_dot(acs, e128_ref[...])
    acs_x = _split3_dot(acs, e64_ref[...])
    dt_x = _split3_dot(dt, e64_ref[...])
    xd = xs * dt_x
    last_x = acs_x[chunk - 1:chunk, :]
    xdw = (xd * jnp.exp(last_x - acs_x)).astype(BF16)
    xdb = xd.astype(BF16)
    ea_x = jnp.exp(acs_x)

    lrow = lax.broadcasted_iota(jnp.int32, (chunk, chunk), 0)
    scol = lax.broadcasted_iota(jnp.int32, (chunk, chunk), 1)
    tril = scol <= lrow
    lane = lax.broadcasted_iota(jnp.int32, (1, LANES), 1)
    first = lane < SSM_HEAD_DIM
    heads_per_group = SSM_HEADS // SSM_GROUPS

    for pair in range(SSM_HEADS // 2):
        g = (2 * pair) // heads_per_group
        bg = bm[:, g * SSM_STATE:(g + 1) * SSM_STATE]
        cg = cm[:, g * SSM_STATE:(g + 1) * SSM_STATE]
        cbm = _dot_nt(cg, bg)
        xd_pair = xdb[:, pair * LANES:(pair + 1) * LANES]
        ys = []
        for e in range(2):
            hd = 2 * pair + e
            seg = acs_col[:, hd * LANES:(hd + 1) * LANES] - acs_t[hd:hd + 1, :]
            decay = jnp.exp(jnp.where(tril, seg, -jnp.inf))
            ys.append(_dot((cbm * decay).astype(BF16), xd_pair))
        y_diag = jnp.where(first, ys[0], ys[1])
        st_pair = st_ref[2 * pair:2 * pair + 2].reshape(2 * SSM_HEAD_DIM, SSM_STATE)
        y_off = _dot_nt(cg, st_pair.astype(BF16)) * ea_x[:, pair * LANES:(pair + 1) * LANES]
        yscr[:, pair * LANES:(pair + 1) * LANES] = y_diag + y_off
        new = _dot_tn(xdw[:, pair * LANES:(pair + 1) * LANES], bg)
        for e in range(2):
            hd = 2 * pair + e
            dec = jnp.exp(acs_col[chunk - 1:chunk, hd * LANES:(hd + 1) * LANES])
            st_ref[hd] = st_ref[hd] * dec + new[e * SSM_HEAD_DIM:(e + 1) * SSM_HEAD_DIM, :]

    y = yscr[...] + xs * dskip_ref[...]
    y = y * _silu(z_ref[...])
    y_ref[...] = _group_rms(y, seg_ref[...], ng_ref[...], d_inner // SSM_GROUPS).astype(y_ref.dtype)


def _ssd_consts(d_inner):
    r = jnp.arange(SSM_CHUNK)
    tri = (r[:, None] >= r[None, :]).astype(BF16)
    return (tri, _expand_matrix(LANES, SSM_HEADS, SSM_HEAD_DIM), _expand_matrix(LANES, SSM_HEADS, LANES),
            _seg_mean_matrix(d_inner // SSM_GROUPS, d_inner // SSM_GROUPS))


def _pad_lanes(v):
    return jnp.pad(v, (0, LANES - v.shape[0])).reshape(1, LANES)


def _ssd_prompt(zx, conv_w, conv_b, dt_bias, a_log, d_skip, norm_g, batch, seq):
    d_inner = SSM_HEADS * SSM_HEAD_DIM
    chunk = SSM_CHUNK
    nc = seq // chunk
    conv_dim = conv_w.shape[1]
    tri, e64, e128, seg = _ssd_consts(d_inner)
    dt_blk = (2 * d_inner + 2 * SSM_GROUPS * SSM_STATE) // LANES
    const = lambda b, c: (0, 0)
    y, st = pl.pallas_call(
        functools.partial(_ssd_prompt_kernel, chunk=chunk),
        grid=(batch, nc),
        in_specs=[
            pl.BlockSpec((chunk, d_inner), lambda b, c: (b * nc + c, 0)),
            pl.BlockSpec((chunk, d_inner), lambda b, c: (b * nc + c, 1)),
            pl.BlockSpec((chunk, d_inner), lambda b, c: (b * nc + c, 2)),
            pl.BlockSpec((chunk, LANES), lambda b, c: (b * nc + c, dt_blk)),
            pl.BlockSpec((SSM_CONV, conv_dim), const),
            pl.BlockSpec((1, conv_dim), const),
            pl.BlockSpec((1, LANES), const),
            pl.BlockSpec((1, LANES), const),
            pl.BlockSpec((1, d_inner), const),
            pl.BlockSpec((1, d_inner), const),
            pl.BlockSpec(tri.shape, const),
            pl.BlockSpec(e64.shape, const),
            pl.BlockSpec(e128.shape, const),
            pl.BlockSpec(seg.shape, const),
        ],
        out_specs=[
            pl.BlockSpec((chunk, d_inner), lambda b, c: (b * nc + c, 0)),
            pl.BlockSpec((None, SSM_HEADS, SSM_HEAD_DIM, SSM_STATE), lambda b, c: (b, 0, 0, 0)),
        ],
        out_shape=[jax.ShapeDtypeStruct((batch * seq, d_inner), BF16),
                   jax.ShapeDtypeStruct((batch, SSM_HEADS, SSM_HEAD_DIM, SSM_STATE), F32)],
        scratch_shapes=[pltpu.VMEM((8, conv_dim), F32), pltpu.VMEM((chunk, d_inner), F32)],
        compiler_params=_params("parallel", "arbitrary"),
        name="ssd_prompt",
    )(zx, zx, zx, zx, conv_w, conv_b.reshape(1, -1), _pad_lanes(dt_bias), _pad_lanes(a_log),
      jnp.repeat(d_skip, SSM_HEAD_DIM).reshape(1, d_inner), norm_g.reshape(1, d_inner), tri, e64, e128, seg)
    return y, st


def _ssd_decode_kernel(z_ref, x_ref, bc_ref, dt_ref, cs_ref, st_ref, cw_ref, cb_ref, dtb_ref, alog_ref,
                       dskip_ref, ng_ref, e64_ref, e128_ref, seg_ref, y_ref, so_ref, *, rb):
    d_inner = x_ref.shape[1]
    gn = SSM_GROUPS * SSM_STATE
    cw = cw_ref[...]
    xbc_raw = jnp.concatenate([x_ref[...], bc_ref[...]], axis=1)
    out = cb_ref[...] + cw[SSM_CONV - 1:SSM_CONV, :] * xbc_raw
    for j in range(SSM_CONV - 1):
        out = out + cw[j:j + 1, :] * cs_ref[j]
    xbc = _silu(out)
    xs = xbc[:, :d_inner]
    bm = xbc[:, d_inner:d_inner + gn]
    cm = xbc[:, d_inner + gn:]

    dt = _softplus(dt_ref[...] + dtb_ref[...])
    a = dt * (-jnp.exp(alog_ref[...]))
    a_col = _split3_dot(a, e128_ref[...])
    dt_x = _split3_dot(dt, e64_ref[...])
    xd = (xs * dt_x).astype(BF16)
    bmb = bm.astype(BF16)
    cmb = cm.astype(BF16)
    rows = lax.broadcasted_iota(jnp.int32, (rb, 1), 0)
    hpg = SSM_HEADS // SSM_GROUPS
    gw = hpg * SSM_HEAD_DIM

    ycols = []
    for g in range(SSM_GROUPS):
        bg = bmb[:, g * SSM_STATE:(g + 1) * SSM_STATE]
        cg = cmb[:, g * SSM_STATE:(g + 1) * SSM_STATE]
        xg = xd[:, g * gw:(g + 1) * gw]
        yg = jnp.zeros((rb, gw), F32)
        for r in range(rb):
            outer = _dot_tn(jnp.where(rows == r, xg, jnp.zeros_like(xg)), bg)
            news = []
            for e in range(hpg):
                hd = g * hpg + e
                dec = jnp.exp(a_col[r:r + 1, hd * LANES:(hd + 1) * LANES])
                new = st_ref[r, hd] * dec + outer[e * SSM_HEAD_DIM:(e + 1) * SSM_HEAD_DIM, :]
                so_ref[r, hd] = new
                news.append(new)
            yr = _dot_nt(cg, jnp.concatenate(news, axis=0).astype(BF16))
            yg = jnp.where(rows == r, yr, yg)
        ycols.append(yg)
    y = jnp.concatenate(ycols, axis=1) + xs * dskip_ref[...]
    y = y * _silu(z_ref[...])
    y_ref[...] = _group_rms(y, seg_ref[...], ng_ref[...], d_inner // SSM_GROUPS).astype(y_ref.dtype)


def _ssd_decode(zx, conv_state, ssm_state, conv_w, conv_b, dt_bias, a_log, d_skip, norm_g):
    r = zx.shape[0]
    d_inner = SSM_HEADS * SSM_HEAD_DIM
    rb = 8
    conv_dim = conv_w.shape[1]
    _, e64, e128, seg = _ssd_consts(d_inner)
    dt_blk = (2 * d_inner + 2 * SSM_GROUPS * SSM_STATE) // LANES
    cs = jnp.transpose(conv_state, (1, 0, 2))
    const = lambda i: (0, 0)
    st_spec = pl.BlockSpec((rb, SSM_HEADS, SSM_HEAD_DIM, SSM_STATE), lambda i: (i, 0, 0, 0))
    y, st = pl.pallas_call(
        functools.partial(_ssd_decode_kernel, rb=rb),
        grid=(r // rb,),
        in_specs=[
            pl.BlockSpec((rb, d_inner), lambda i: (i, 0)),
            pl.BlockSpec((rb, d_inner), lambda i: (i, 1)),
            pl.BlockSpec((rb, d_inner), lambda i: (i, 2)),
            pl.BlockSpec((rb, LANES), lambda i: (i, dt_blk)),
            pl.BlockSpec((SSM_CONV - 1, rb, conv_dim), lambda i: (0, i, 0)),
            st_spec,
            pl.BlockSpec((SSM_CONV, conv_dim), const),
            pl.BlockSpec((1, conv_dim), const),
            pl.BlockSpec((1, LANES), const),
            pl.BlockSpec((1, LANES), const),
            pl.BlockSpec((1, d_inner), const),
            pl.BlockSpec((1, d_inner), const),
            pl.BlockSpec(e64.shape, const),
            pl.BlockSpec(e128.shape, const),
            pl.BlockSpec(seg.shape, const),
        ],
        out_specs=[pl.BlockSpec((rb, d_inner), lambda i: (i, 0)), st_spec],
        out_shape=[jax.ShapeDtypeStruct((r, d_inner), BF16),
                   jax.ShapeDtypeStruct(ssm_state.shape, F32)],
        compiler_params=_params("parallel"),
        name="ssd_decode",
    )(zx, zx, zx, zx, cs, ssm_state, conv_w, conv_b.reshape(1, -1), _pad_lanes(dt_bias), _pad_lanes(a_log),
      jnp.repeat(d_skip, SSM_HEAD_DIM).reshape(1, d_inner), norm_g.reshape(1, d_inner), e64, e128, seg)
    return y, st


def kernel(x_prompt, x_sample, cache_sb_k, cache_sb_v, cache_diff_k, cache_diff_v, state_ssm_conv, state_ssm,
           page_table, ffn_norm, ffn_w_gate, ffn_w_up, ffn_w_down, mix_norm, sb_w_qkv, sb_w_o,
           diff_w_qkv, diff_q_norm, diff_k_norm, diff_lambda_q1, diff_lambda_k1, diff_lambda_q2, diff_lambda_k2,
           diff_subln, diff_w_o, ssm_w_in, ssm_conv_w, ssm_conv_b, ssm_dt_bias, ssm_a_log, ssm_d, ssm_norm,
           ssm_w_out):
    bp, seq, d = x_prompt.shape
    bs = x_sample.shape[0]
    depth = ffn_norm.shape[0]
    xp = x_prompt.reshape(bp * seq, d)
    xs = x_sample.reshape(bs, d)
    wg, wu, wd = ffn_w_gate.astype(BF16), ffn_w_up.astype(BF16), ffn_w_down.astype(BF16)

    sb_kp, sb_vp, sb_ks, sb_vs = [], [], [], []
    d_kp, d_vp, d_ks, d_vs = [], [], [], []
    cv_p, ss_p, cv_s, ss_s = [], [], [], []
    for i in range(depth):
        kind, j = i % N_MIXERS, i // N_MIXERS
        xp = _ffn(xp, ffn_norm[i, 0], wg[i, 0], wu[i, 0], wd[i, 0])
        xs = _ffn(xs, ffn_norm[i, 0], wg[i, 0], wu[i, 0], wd[i, 0])
        if kind == 0:
            w = sb_w_qkv[j].astype(BF16)
            wq, wk, wv = w[:, :d], w[:, d:2 * d], w[:, 2 * d:]
            wo = sb_w_o[j].astype(BF16)
            q, k, v = _qkv(xp, mix_norm[i], wq, wk, wv)
            sb_kp.append(k.reshape(bp, seq, SB_HEADS, SB_HEAD_DIM))
            sb_vp.append(v.reshape(bp, seq, SB_HEADS, SB_HEAD_DIM))
            xp = _oproj(_sb_prompt(q, k, v, bp, seq), wo, xp)
            q, k, v = _qkv(xs, mix_norm[i], wq, wk, wv)
            sb_ks.append(k.reshape(bs, 1, SB_HEADS, SB_HEAD_DIM))
            sb_vs.append(v.reshape(bs, 1, SB_HEADS, SB_HEAD_DIM))
            xs = _oproj(_sb_decode(q, cache_sb_k, cache_sb_v, j, page_table), wo, xs)
        elif kind == 1:
            lam_init = 0.8 - 0.6 * math.exp(-0.3 * i)
            w = diff_w_qkv[j].astype(BF16)
            wq, wk, wv = w[:, :d], w[:, d:2 * d], w[:, 2 * d:]
            wo = diff_w_o[j].astype(BF16)
            qg = jnp.tile(diff_q_norm[j], d // DIFF_HEAD_DIM)
            kg = jnp.tile(diff_k_norm[j], d // DIFF_HEAD_DIM)
            lam_params = jnp.stack([diff_lambda_q1[j], diff_lambda_k1[j], diff_lambda_q2[j], diff_lambda_k2[j]])
            q, k, v = _qkv(xp, mix_norm[i], wq, wk, wv, qg, kg)
            d_kp.append(k.reshape(bp, seq, 2, DIFF_HEADS, DIFF_HEAD_DIM))
            d_vp.append(v.reshape(bp, seq, DIFF_HEADS, 2 * DIFF_HEAD_DIM))
            o = _diff_prompt(q, k, v, lam_params, diff_subln[j], lam_init, bp, seq)
            xp = _oproj(o, wo, xp)
            q, k, v = _qkv(xs, mix_norm[i], wq, wk, wv, qg, kg)
            d_ks.append(k.reshape(bs, 1, 2, DIFF_HEADS, DIFF_HEAD_DIM))
            d_vs.append(v.reshape(bs, 1, DIFF_HEADS, 2 * DIFF_HEAD_DIM))
            o = _diff_decode(q, k, v, cache_diff_k, cache_diff_v, j, page_table, lam_params, diff_subln[j],
                             lam_init)
            xs = _oproj(o, wo, xs)
        else:
            d_inner = SSM_HEADS * SSM_HEAD_DIM
            conv_dim = ssm_conv_w.shape[2]
            w_in = jnp.pad(ssm_w_in[j], ((0, 0), (0, LANES - SSM_HEADS))).astype(BF16)
            wo = ssm_w_out[j].astype(BF16)
            sp = (ssm_conv_w[j], ssm_conv_b[j], ssm_dt_bias[j], ssm_a_log[j], ssm_d[j], ssm_norm[j])
            tn = _tile(w_in.shape[1], 896)
            zx = _norm_matmul(xp, mix_norm[i], w_in, tn)
            y, st = _ssd_prompt(zx, *sp, bp, seq)
            cv_p.append(zx.reshape(bp, seq, -1)[:, seq - (SSM_CONV - 1):, d_inner:d_inner + conv_dim])
            ss_p.append(st)
            xp = _oproj(y, wo, xp)
            zx = _norm_matmul(xs, mix_norm[i], w_in, tn)
            y, st = _ssd_decode(zx, state_ssm_conv[j], state_ssm[j], *sp)
            cv_s.append(jnp.concatenate([state_ssm_conv[j][:, 1:], zx[:, None, d_inner:d_inner + conv_dim]], axis=1))
            ss_s.append(st)
            xs = _oproj(y, wo, xs)
        xp = _ffn(xp, ffn_norm[i, 1], wg[i, 1], wu[i, 1], wd[i, 1])
        xs = _ffn(xs, ffn_norm[i, 1], wg[i, 1], wu[i, 1], wd[i, 1])
    return (xp.reshape(bp, seq, d), xs.reshape(bs, 1, d),
            jnp.stack(sb_kp), jnp.stack(sb_vp), jnp.stack(sb_ks), jnp.stack(sb_vs),
            jnp.stack(d_kp), jnp.stack(d_vp), jnp.stack(d_ks), jnp.stack(d_vs),
            jnp.stack(cv_p), jnp.stack(ss_p), jnp.stack(cv_s), jnp.stack(ss_s))
```

```python
import functools
import math

import jax
import jax.numpy as jnp
from jax import lax
from jax.experimental import pallas as pl
from jax.experimental.pallas import tpu as pltpu

F32 = jnp.float32
BF16 = jnp.bfloat16

NORM_EPS = 1e-6
N_MIXERS = 3
SB_HEADS = 16
SB_HEAD_DIM = 64
DIFF_HEADS = 8
DIFF_HEAD_DIM = 64
SSM_HEAD_DIM = 64
SSM_HEADS = 32
SSM_GROUPS = 8
SSM_STATE = 128
SSM_CONV = 4
SSM_CHUNK = 128
LANES = 128
VMEM_LIMIT = 48 * 1024 * 1024


def _params(*sem):
    return pltpu.CompilerParams(dimension_semantics=sem, vmem_limit_bytes=VMEM_LIMIT)


def _tile(n, pref):
    t = min(n, pref)
    while n % t:
        t //= 2
    return t


def _dot(a, b):
    return jnp.dot(a, b, preferred_element_type=F32)


def _dot_nt(a, b):
    return lax.dot_general(a, b, (((1,), (1,)), ((), ())), preferred_element_type=F32)


def _dot_tn(a, b):
    return lax.dot_general(a, b, (((0,), (0,)), ((), ())), preferred_element_type=F32)


def _split_dot(x, m):
    hi = x.astype(BF16)
    lo = (x - hi.astype(F32)).astype(BF16)
    return _dot(hi, m) + _dot(lo, m)


def _split_dot_left(m, x):
    hi = x.astype(BF16)
    lo = (x - hi.astype(F32)).astype(BF16)
    return _dot(m, hi) + _dot(m, lo)


def _rms_rows(x, g):
    return x * lax.rsqrt(jnp.mean(x * x, axis=-1, keepdims=True) + NORM_EPS) * g


def _silu(x):
    return x / (1.0 + jnp.exp(-x))


def _softplus(z):
    return jnp.maximum(z, 0.0) + jnp.log(1.0 + jnp.exp(-jnp.abs(z)))


FFN_CHUNK = 512


def _ffn_kernel(x_ref, g_ref, wg_ref, wu_ref, wd_ref, o_ref, acc_scr):
    x = x_ref[...]
    h = _rms_rows(x, g_ref[...]).astype(BF16)
    dff = wg_ref.shape[1]
    for n, lo in enumerate(range(0, dff, FFN_CHUNK)):
        hi = min(lo + FFN_CHUNK, dff)
        gate = _dot(h, wg_ref[:, lo:hi])
        up = _dot(h, wu_ref[:, lo:hi])
        down = _dot((_silu(gate) * up).astype(BF16), wd_ref[lo:hi, :])
        if n == 0:
            acc_scr[...] = down
        else:
            acc_scr[...] += down
    o_ref[...] = x + 0.5 * acc_scr[...]


def _ffn(x, g, wg, wu, wd):
    m, d = x.shape
    dff = wg.shape[1]
    tm = _tile(m, 1024)
    resident = dict(pipeline_mode=pl.Buffered(1))
    return pl.pallas_call(
        _ffn_kernel,
        grid=(m // tm,),
        in_specs=[
            pl.BlockSpec((tm, d), lambda i: (i, 0)),
            pl.BlockSpec((1, d), lambda i: (0, 0)),
            pl.BlockSpec((d, dff), lambda i: (0, 0), **resident),
            pl.BlockSpec((d, dff), lambda i: (0, 0), **resident),
            pl.BlockSpec((dff, d), lambda i: (0, 0), **resident),
        ],
        out_specs=pl.BlockSpec((tm, d), lambda i: (i, 0)),
        out_shape=jax.ShapeDtypeStruct((m, d), F32),
        scratch_shapes=[pltpu.VMEM((tm, d), F32)],
        compiler_params=_params("parallel"),
        name="ffn",
    )(x, g.reshape(1, d), wg, wu, wd)


def _norm_matmul_kernel(x_ref, g_ref, w_ref, o_ref, h_scr):
    @pl.when(pl.program_id(1) == 0)
    def _():
        h_scr[...] = _rms_rows(x_ref[...], g_ref[...]).astype(BF16)

    o_ref[...] = _dot(h_scr[...], w_ref[...])


def _norm_matmul(x, g, w, tn):
    m, d = x.shape
    n = w.shape[1]
    tm = _tile(m, 1024)
    return pl.pallas_call(
        _norm_matmul_kernel,
        grid=(m // tm, n // tn),
        in_specs=[
            pl.BlockSpec((tm, d), lambda i, j: (i, 0)),
            pl.BlockSpec((1, d), lambda i, j: (0, 0)),
            pl.BlockSpec((d, tn), lambda i, j: (0, j)),
        ],
        out_specs=pl.BlockSpec((tm, tn), lambda i, j: (i, j)),
        out_shape=jax.ShapeDtypeStruct((m, n), F32),
        scratch_shapes=[pltpu.VMEM((tm, d), BF16)],
        compiler_params=_params("parallel", "arbitrary"),
        name="norm_matmul",
    )(x, g.reshape(1, d), w)


def _head_rms(x, seg, g):
    cols = []
    for c in range(x.shape[1] // LANES):
        xc = x[:, c * LANES:(c + 1) * LANES]
        ms = _split_dot(xc * xc, seg)
        cols.append(xc * lax.rsqrt(ms + NORM_EPS))
    return jnp.concatenate(cols, axis=1) * g


def _qkv_kernel(x_ref, g_ref, wq_ref, wk_ref, wv_ref, seg_ref, qg_ref, kg_ref,
                q_ref, k_ref, v_ref, *t_refs, qk_norm):
    h = _rms_rows(x_ref[...], g_ref[...]).astype(BF16)
    q = _dot(h, wq_ref[...])
    k = _dot(h, wk_ref[...])
    if qk_norm:
        q = _head_rms(q, seg_ref[...], qg_ref[...])
        k = _head_rms(k, seg_ref[...], kg_ref[...])
    v = _dot(h, wv_ref[...])
    q_ref[...] = q
    k_ref[...] = k
    v_ref[...] = v
    for t_ref, val in zip(t_refs, (k, v)):
        t_ref[...] = val.T


def _seg_mean_matrix(n, group):
    r = jnp.arange(n)
    return jnp.where((r[:, None] // group) == (r[None, :] // group), 1.0 / group, 0.0).astype(BF16)


def _qkv(x, g, w, batch, n_transposed, qg=None, kg=None):
    m, d = x.shape
    seq = m // batch
    tm = _tile(seq, 512)
    nt = seq // tm
    qk_norm = qg is not None
    if not qk_norm:
        qg = kg = jnp.ones((d,), F32)
    row = lambda i: (i, 0)
    const = lambda i: (0, 0)
    out = jax.ShapeDtypeStruct((m, d), F32)
    t_spec = pl.BlockSpec((None, d, tm), lambda i: (i // nt, 0, i % nt))
    t_out = jax.ShapeDtypeStruct((batch, d, seq), F32)
    return pl.pallas_call(
        functools.partial(_qkv_kernel, qk_norm=qk_norm),
        grid=(m // tm,),
        in_specs=[
            pl.BlockSpec((tm, d), row),
            pl.BlockSpec((1, d), const),
            pl.BlockSpec((d, d), lambda i: (0, 0)),
            pl.BlockSpec((d, d), lambda i: (0, 1)),
            pl.BlockSpec((d, d), lambda i: (0, 2)),
            pl.BlockSpec((LANES, LANES), const),
            pl.BlockSpec((1, d), const),
            pl.BlockSpec((1, d), const),
        ],
        out_specs=[pl.BlockSpec((tm, d), row)] * 3 + [t_spec] * n_transposed,
        out_shape=[out, out, out] + [t_out] * n_transposed,
        compiler_params=_params("parallel"),
        name="qkv_norm" if qk_norm else "qkv",
    )(x, g.reshape(1, d), w, w, w, _seg_mean_matrix(LANES, DIFF_HEAD_DIM),
      qg.reshape(1, d), kg.reshape(1, d))


def _oproj_kernel(o_ref, w_ref, r_ref, y_ref):
    y_ref[...] = r_ref[...] + _dot(o_ref[...].astype(BF16), w_ref[...])


def _oproj(o, w, res):
    m, k = o.shape
    d = w.shape[1]
    tm = _tile(m, 512)
    return pl.pallas_call(
        _oproj_kernel,
        grid=(m // tm,),
        in_specs=[
            pl.BlockSpec((tm, k), lambda i: (i, 0)),
            pl.BlockSpec((k, d), lambda i: (0, 0)),
            pl.BlockSpec((tm, d), lambda i: (i, 0)),
        ],
        out_specs=pl.BlockSpec((tm, d), lambda i: (i, 0)),
        out_shape=jax.ShapeDtypeStruct((m, d), F32),
        compiler_params=_params("parallel"),
        name="oproj",
    )(o, w, res)


def _suffix_matrix(n):
    r = jnp.arange(n)
    return (r[:, None] > r[None, :]).astype(BF16)


def _sb_prompt_kernel(q_ref, k_ref, v_ref, u_ref, o_ref, kb_scr, vb_scr, acc_scr, *, tk, halves):
    qi = pl.program_id(2)

    @pl.when(qi == 0)
    def _():
        kb_scr[...] = k_ref[...].astype(BF16)
        vb_scr[...] = v_ref[...].astype(BF16)

    q2 = q_ref[...] * (SB_HEAD_DIM ** -0.5)
    lane = lax.broadcasted_iota(jnp.int32, (1, LANES), 1)
    row = lax.broadcasted_iota(jnp.int32, (tk, tk), 0)
    col = lax.broadcasted_iota(jnp.int32, (tk, tk), 1)
    causal = col < row
    u = u_ref[...]
    base = qi * halves
    chains = [(hh, a) for hh in range(2) for a in range(halves)]
    qms = []
    for hh, a in chains:
        in_head = (lane // SB_HEAD_DIM) == hh
        qms.append(jnp.where(in_head, q2[a * tk:(a + 1) * tk], 0.0).astype(BF16))

    def tiles(cs, kjs, rs, masked):
        starts = [pl.multiple_of(kj * tk, tk) for kj in kjs]
        zs = [_dot_nt(qms[c], kb_scr[pl.ds(st, tk), :]) for c, st in zip(cs, starts)]
        sps = [_softplus(z) for z in zs]
        if masked:
            sps = [jnp.where(causal, sp, 0.0) for sp in sps]
        cums = [_dot(sp.astype(BF16), u) for sp in sps]
        ws = [jnp.exp(z - sp - cum - r) for z, sp, cum, r in zip(zs, sps, cums, rs)]
        if masked:
            ws = [jnp.where(causal, w, 0.0) for w in ws]
        for c, st, w in zip(cs, starts, ws):
            pv = _dot(w.astype(BF16), vb_scr[pl.ds(st, tk), :])
            if masked:
                acc_scr[c] = pv
            else:
                acc_scr[c] += pv
        return [r + cum[:, :1] + sp[:, :1] for r, cum, sp in zip(rs, cums, sps)]

    n_chain = len(chains)
    rs = [jnp.zeros((tk, 1), F32)] * n_chain
    for t in range(halves):
        cs = [c for c in range(n_chain) if chains[c][1] >= t]
        new = tiles(cs, [base + chains[c][1] - t for c in cs], [rs[c] for c in cs], t == 0)
        for c, r in zip(cs, new):
            rs[c] = r

    depth = 2 if halves % 2 == 0 else 1

    def body(j, rs):
        kjs = [base - 1 - depth * j - t for t in range(depth)]
        starts = [pl.multiple_of(kj * tk, tk) for kj in kjs]
        zs = [[_dot_nt(qms[c], kb_scr[pl.ds(st, tk), :]) for c in range(n_chain)] for st in starts]
        sps = [[_softplus(z) for z in zt] for zt in zs]
        cums = [[_dot(sp.astype(BF16), u) for sp in spt] for spt in sps]
        rs = list(rs)
        for c in range(n_chain):
            pv = None
            for t in range(depth):
                w = jnp.exp(zs[t][c] - sps[t][c] - cums[t][c] - rs[c]).astype(BF16)
                d = _dot(w, vb_scr[pl.ds(starts[t], tk), :])
                pv = d if pv is None else pv + d
                rs[c] = rs[c] + cums[t][c][:, :1] + sps[t][c][:, :1]
            acc_scr[c] += pv
        return tuple(rs)

    lax.fori_loop(0, base // depth, body, tuple(rs))
    for a in range(halves):
        o_ref[a * tk:(a + 1) * tk, :] = jnp.where(
            lane < SB_HEAD_DIM, acc_scr[a], acc_scr[halves + a]).astype(o_ref.dtype)


ATTN_TK = 256
ATTN_HALVES = 2


def _attn_tiles(seq):
    tk = _tile(seq, ATTN_TK)
    halves = ATTN_HALVES if seq % (ATTN_HALVES * tk) == 0 else 1
    return tk, halves


def _sb_prompt(q, k, v, batch, seq):
    m, d = q.shape
    tk, halves = _attn_tiles(seq)
    tq = tk * halves
    nq = seq // tq
    pairs = d // LANES
    k3 = k.reshape(batch, seq, d)
    v3 = v.reshape(batch, seq, d)
    return pl.pallas_call(
        functools.partial(_sb_prompt_kernel, tk=tk, halves=halves),
        grid=(batch, pairs, nq),
        in_specs=[
            pl.BlockSpec((tq, LANES), lambda b, p, i: (b * nq + i, p)),
            pl.BlockSpec((None, seq, LANES), lambda b, p, i: (b, 0, p)),
            pl.BlockSpec((None, seq, LANES), lambda b, p, i: (b, 0, p)),
            pl.BlockSpec((tk, tk), lambda b, p, i: (0, 0)),
        ],
        out_specs=pl.BlockSpec((tq, LANES), lambda b, p, i: (b * nq + i, p)),
        out_shape=jax.ShapeDtypeStruct((m, d), BF16),
        scratch_shapes=[pltpu.VMEM((seq, LANES), BF16), pltpu.VMEM((seq, LANES), BF16),
                        pltpu.VMEM((2 * halves, tk, LANES), F32)],
        compiler_params=_params("parallel", "parallel", "arbitrary"),
        name="sb_prompt",
    )(q, k3, v3, _suffix_matrix(tk))


DECODE_PAGES_PER_STEP = 4


def _pages_per_step(n_pages):
    pps = DECODE_PAGES_PER_STEP
    while n_pages % pps:
        pps //= 2
    return pps


def _slot_minor(cache):
    nd = cache.ndim
    t = jnp.transpose(cache, (0, 1) + tuple(range(3, nd)) + (2,))
    return t.reshape(cache.shape[0], cache.shape[1], -1, cache.shape[2])


def _head_rows(x_row, n_rows):
    width = x_row.shape[1] // n_rows
    r = lax.broadcasted_iota(jnp.int32, (n_rows, n_rows * width), 0)
    c = lax.broadcasted_iota(jnp.int32, (n_rows, n_rows * width), 1)
    own = (c // width) == r
    return jnp.where(own, x_row, 0.0), own


def _sb_decode_kernel(pt_ref, q_ref, *refs, pps):
    k_refs, v_refs = refs[:pps], refs[pps:2 * pps]
    u_ref, o_ref, q_scr, acc_scr, r_scr = refs[2 * pps:]
    p = pl.program_id(1)

    @pl.when(p == 0)
    def _():
        qrows, _ = _head_rows(q_ref[0] * (SB_HEAD_DIM ** -0.5), SB_HEADS)
        q_scr[...] = qrows.astype(BF16)
        acc_scr[...] = jnp.zeros_like(acc_scr)
        r_scr[...] = jnp.zeros_like(r_scr)

    q = q_scr[...]
    r = r_scr[...]
    acc = acc_scr[...]
    zts = [_dot(q, k_ref[...].astype(BF16)) for k_ref in k_refs]
    sps = [_softplus(zt) for zt in zts]
    cums = [_split_dot(sp, u_ref[...]) for sp in sps]
    for zt, sp, cum, v_ref in zip(zts, sps, cums, v_refs):
        wb = jnp.exp(zt - sp - cum - r).astype(BF16)
        acc = acc + _dot_nt(wb, v_ref[...].astype(BF16))
        r = r + cum[:, :1] + sp[:, :1]
    acc_scr[...] = acc
    r_scr[...] = r

    @pl.when(p == pl.num_programs(1) - 1)
    def _():
        _, own = _head_rows(jnp.zeros((1, acc.shape[1]), F32), SB_HEADS)
        o_ref[0] = jnp.sum(jnp.where(own, acc, 0.0), axis=0, keepdims=True)


def _sb_decode(q, cache_k, cache_v, layer, page_table):
    r, d = q.shape
    n_pages = page_table.shape[1]
    page = cache_k.shape[2]
    pps = _pages_per_step(n_pages)
    pt = page_table.reshape(-1)

    cache_k, cache_v = _slot_minor(cache_k), _slot_minor(cache_v)

    def kv_map(i):
        return lambda b, p, pt_ref: (layer, pt_ref[b * n_pages + n_pages - 1 - (p * pps + i)], 0, 0)

    kv_specs = [pl.BlockSpec((None, None, d, page), kv_map(i)) for i in range(pps)]
    row_map = lambda b, p, pt_ref: (b, 0, 0)
    out = pl.pallas_call(
        functools.partial(_sb_decode_kernel, pps=pps),
        grid_spec=pltpu.PrefetchScalarGridSpec(
            num_scalar_prefetch=1,
            grid=(r, n_pages // pps),
            in_specs=[pl.BlockSpec((1, 1, d), row_map)] + kv_specs + kv_specs
            + [pl.BlockSpec((page, page), lambda b, p, pt_ref: (0, 0))],
            out_specs=pl.BlockSpec((1, 1, d), row_map),
            scratch_shapes=[pltpu.VMEM((SB_HEADS, d), BF16), pltpu.VMEM((SB_HEADS, d), F32),
                            pltpu.VMEM((SB_HEADS, 1), F32)],
        ),
        out_shape=jax.ShapeDtypeStruct((r, 1, d), F32),
        compiler_params=_params("parallel", "arbitrary"),
        name="sb_decode",
    )(pt, q.reshape(r, 1, d), *([cache_k] * pps), *([cache_v] * pps), _suffix_matrix(page))
    return out.reshape(r, d)


def _lambda(lq1, lk1, lq2, lk2, lam_init):
    return (jnp.exp(jnp.sum(lq1 * lk1, axis=-1, keepdims=True))
            - jnp.exp(jnp.sum(lq2 * lk2, axis=-1, keepdims=True)) + lam_init)


def _diff_prompt_kernel(slope_ref, q0_ref, q1_ref, k0_ref, k1_ref, v_ref, lam_ref, sg_ref, o_ref,
                        k0_scr, k1_scr, vb_scr, acc_scr, *, tk, halves, lam_init):
    h = pl.program_id(1)
    qi = pl.program_id(2)

    @pl.when(qi == 0)
    def _():
        k0_scr[...] = k0_ref[...].astype(BF16)
        k1_scr[...] = k1_ref[...].astype(BF16)
        vb_scr[...] = v_ref[...].astype(BF16)

    slope = slope_ref[h]
    lane = lax.broadcasted_iota(jnp.int32, (1, LANES), 1)
    in_head = (lane // DIFF_HEAD_DIM) == (h % 2)
    row = lax.broadcasted_iota(jnp.int32, (tk, tk), 0)
    col = lax.broadcasted_iota(jnp.int32, (tk, tk), 1)
    causal = col <= row
    rel = lax.broadcasted_iota(jnp.int32, (1, tk), 1).astype(F32)
    base = qi * halves
    k_scrs = (k0_scr, k1_scr)
    chains = [(c, a) for c in range(2) for a in range(halves)]
    qms = []
    for c, a in chains:
        q = (q0_ref, q1_ref)[c][a * tk:(a + 1) * tk, :]
        qms.append(jnp.where(in_head, q * (DIFF_HEAD_DIM ** -0.5), 0.0).astype(BF16))

    def tiles(cs, kjs, states, diag):
        starts = [pl.multiple_of(kj * tk, tk) for kj in kjs]
        ss = [_dot_nt(qms[ci], k_scrs[chains[ci][0]][pl.ds(st, tk), :])
              + slope * (rel + ((kj - base) * tk).astype(F32)) for ci, st, kj in zip(cs, starts, kjs)]
        if diag:
            ss = [jnp.where(causal, s, -jnp.inf) for s in ss]
            ms = [jnp.max(s, axis=-1, keepdims=True) for s in ss]
        else:
            ms = [jnp.maximum(m, jnp.max(s, axis=-1, keepdims=True)) for (m, _), s in zip(states, ss)]
        ps = [jnp.exp(s - m) for s, m in zip(ss, ms)]
        out = []
        for i, (ci, st, p) in enumerate(zip(cs, starts, ps)):
            pv = _dot(p.astype(BF16), vb_scr[pl.ds(st, tk), :])
            psum = jnp.sum(p, axis=-1, keepdims=True)
            if diag:
                acc_scr[ci] = pv
                out.append((ms[i], psum))
            else:
                alpha = jnp.exp(states[i][0] - ms[i])
                acc_scr[ci] = alpha * acc_scr[ci] + pv
                out.append((ms[i], alpha * states[i][1] + psum))
        return out

    n_chain = len(chains)
    states = [None] * n_chain
    for t in range(halves):
        cs = [ci for ci in range(n_chain) if chains[ci][1] >= t]
        new = tiles(cs, [base + chains[ci][1] - t for ci in cs], [states[ci] for ci in cs], t == 0)
        for ci, st in zip(cs, new):
            states[ci] = st

    def body(j, states):
        return tuple(tiles(list(range(n_chain)), [j] * n_chain, list(states), False))

    states = lax.fori_loop(0, base, body, tuple(states))
    lam = _lambda(lam_ref[0:1, :], lam_ref[1:2, :], lam_ref[2:3, :], lam_ref[3:4, :], lam_init)
    for a in range(halves):
        o = acc_scr[a] / states[a][1] - lam * (acc_scr[halves + a] / states[halves + a][1])
        o = o * lax.rsqrt(jnp.mean(o * o, axis=-1, keepdims=True) + NORM_EPS) * sg_ref[...]
        o_ref[a * tk:(a + 1) * tk, :] = (o * (1.0 - lam_init)).astype(o_ref.dtype)


def _alibi_slopes(n_heads):
    return jnp.exp2(-8.0 * jnp.arange(1, n_heads + 1, dtype=F32) / n_heads)


def _diff_prompt(q, k, v, lam_params, subln_g, lam_init, batch, seq):
    m, d = q.shape
    tk, halves = _attn_tiles(seq)
    tq = tk * halves
    nq = seq // tq
    half = d // (2 * LANES)
    k3 = k.reshape(batch, seq, d)
    v3 = v.reshape(batch, seq, d)
    return pl.pallas_call(
        functools.partial(_diff_prompt_kernel, tk=tk, halves=halves, lam_init=lam_init),
        grid_spec=pltpu.PrefetchScalarGridSpec(
            num_scalar_prefetch=0,
            grid=(batch, DIFF_HEADS, nq),
            in_specs=[
                pl.BlockSpec(memory_space=pltpu.SMEM),
                pl.BlockSpec((tq, LANES), lambda b, h, i: (b * nq + i, h // 2)),
                pl.BlockSpec((tq, LANES), lambda b, h, i: (b * nq + i, half + h // 2)),
                pl.BlockSpec((None, seq, LANES), lambda b, h, i: (b, 0, h // 2)),
                pl.BlockSpec((None, seq, LANES), lambda b, h, i: (b, 0, half + h // 2)),
                pl.BlockSpec((None, seq, LANES), lambda b, h, i: (b, 0, h)),
                pl.BlockSpec((4, DIFF_HEAD_DIM), lambda b, h, i: (0, 0)),
                pl.BlockSpec((1, LANES), lambda b, h, i: (0, 0)),
            ],
            out_specs=pl.BlockSpec((tq, LANES), lambda b, h, i: (b * nq + i, h)),
            scratch_shapes=[pltpu.VMEM((seq, LANES), BF16)] * 3 + [pltpu.VMEM((2 * halves, tk, LANES), F32)],
        ),
        out_shape=jax.ShapeDtypeStruct((m, d), BF16),
        compiler_params=_params("parallel", "parallel", "arbitrary"),
        name="diff_prompt",
    )(_alibi_slopes(DIFF_HEADS), q, q, k3, k3, v3, lam_params, subln_g.reshape(1, LANES))


def _diff_decode_kernel(pt_ref, q_ref, kn_ref, vn_ref, *refs, pps, lam_init, past):
    k_refs, v_refs = refs[:pps], refs[pps:2 * pps]
    slope_ref, lam_ref, sg_ref, rep_ref, o_ref, q_scr, acc_scr, m_scr, l_scr = refs[2 * pps:]
    p = pl.program_id(1)
    nrow = 2 * DIFF_HEADS
    page = k_refs[0].shape[1]

    @pl.when(p == 0)
    def _():
        qb = (q_ref[0] * (DIFF_HEAD_DIM ** -0.5)).astype(BF16).astype(F32)
        qrows, _ = _head_rows(qb, nrow)
        q_scr[...] = qrows.astype(BF16)
        kn = kn_ref[0].astype(BF16).astype(F32)
        m_scr[...] = jnp.sum(qrows * kn, axis=-1, keepdims=True)
        l_scr[...] = jnp.ones_like(l_scr)
        vn = vn_ref[0].astype(BF16).astype(F32)
        acc_scr[...] = jnp.concatenate([vn, vn], axis=0)

    q = q_scr[...]
    big = (nrow, page * DIFF_HEADS)
    own_head = (lax.broadcasted_iota(jnp.int32, big, 1) % DIFF_HEADS
                == lax.broadcasted_iota(jnp.int32, big, 0) % DIFF_HEADS)
    m, l, acc = m_scr[...], l_scr[...], acc_scr[...]
    ss = []
    for i, k_ref in enumerate(k_refs):
        pos = (p * pps + i) * page + lax.broadcasted_iota(jnp.int32, (1, page), 1)
        ss.append(_dot(q, k_ref[...].astype(BF16)) - slope_ref[...] * (past - pos).astype(F32))
    m_new = m
    for s in ss:
        m_new = jnp.maximum(m_new, jnp.max(s, axis=-1, keepdims=True))
    alpha = jnp.exp(m - m_new)
    prs = [jnp.exp(s - m_new) for s in ss]
    l = alpha * l
    acc = alpha * acc
    for pr, v_ref in zip(prs, v_refs):
        l = l + jnp.sum(pr, axis=-1, keepdims=True)
        spread = _dot(pr.astype(BF16), rep_ref[...])
        pbig = jnp.where(own_head, spread, 0.0).astype(BF16)
        acc = acc + _dot(pbig, v_ref[...].astype(BF16))
    m = m_new
    m_scr[...] = m
    l_scr[...] = l
    acc_scr[...] = acc

    @pl.when(p == pl.num_programs(1) - 1)
    def _():
        a = acc / l
        lam = _lambda(lam_ref[0:1, :], lam_ref[1:2, :], lam_ref[2:3, :], lam_ref[3:4, :], lam_init)
        dd = a[:DIFF_HEADS] - lam * a[DIFF_HEADS:]
        dn = dd * lax.rsqrt(jnp.mean(dd * dd, axis=-1, keepdims=True) + NORM_EPS)
        o_ref[0] = dn * sg_ref[...] * (1.0 - lam_init)


def _diff_decode(q, k_new, v_new, cache_k, cache_v, layer, page_table, lam_params, subln_g, lam_init):
    r, d = q.shape
    n_pages = page_table.shape[1]
    page = cache_k.shape[2]
    pps = _pages_per_step(n_pages)
    nrow = 2 * DIFF_HEADS
    vdim = 2 * DIFF_HEAD_DIM
    pt = page_table.reshape(-1)

    def page_of(i):
        return lambda b, p, pt_ref: pt_ref[b * n_pages + p * pps + i]

    cache_k = _slot_minor(cache_k)
    cache_v = cache_v.reshape(cache_v.shape[:2] + (page * DIFF_HEADS, vdim))
    rep = _expand_matrix(page, page, DIFF_HEADS)
    k_specs = [pl.BlockSpec((None, None, d, page),
                            lambda b, p, pt_ref, f=page_of(i): (layer, f(b, p, pt_ref), 0, 0))
               for i in range(pps)]
    v_specs = [pl.BlockSpec((None, None, page * DIFF_HEADS, vdim),
                            lambda b, p, pt_ref, f=page_of(i): (layer, f(b, p, pt_ref), 0, 0))
               for i in range(pps)]
    row_map = lambda b, p, pt_ref: (b, 0, 0)
    const = lambda b, p, pt_ref: (0, 0)
    slopes = jnp.tile(_alibi_slopes(DIFF_HEADS), 2).reshape(nrow, 1)
    out = pl.pallas_call(
        functools.partial(_diff_decode_kernel, pps=pps, lam_init=lam_init, past=n_pages * page),
        grid_spec=pltpu.PrefetchScalarGridSpec(
            num_scalar_prefetch=1,
            grid=(r, n_pages // pps),
            in_specs=[
                pl.BlockSpec((1, 1, d), row_map),
                pl.BlockSpec((1, 1, d), row_map),
                pl.BlockSpec((1, DIFF_HEADS, vdim), row_map),
            ] + k_specs + v_specs + [
                pl.BlockSpec((nrow, 1), const),
                pl.BlockSpec((4, DIFF_HEAD_DIM), const),
                pl.BlockSpec((1, vdim), const),
                pl.BlockSpec(rep.shape, const),
            ],
            out_specs=pl.BlockSpec((1, DIFF_HEADS, vdim), row_map),
            scratch_shapes=[pltpu.VMEM((nrow, d), BF16), pltpu.VMEM((nrow, vdim), F32),
                            pltpu.VMEM((nrow, 1), F32), pltpu.VMEM((nrow, 1), F32)],
        ),
        out_shape=jax.ShapeDtypeStruct((r, DIFF_HEADS, vdim), F32),
        compiler_params=_params("parallel", "arbitrary"),
        name="diff_decode",
    )(pt, q.reshape(r, 1, d), k_new.reshape(r, 1, d), v_new.reshape(r, DIFF_HEADS, vdim),
      *([cache_k] * pps), *([cache_v] * pps), slopes, lam_params, subln_g.reshape(1, vdim), rep)
    return out.reshape(r, d)


def _expand_matrix(n_in, n_heads, width):
    r = jnp.arange(n_in)[:, None]
    c = jnp.arange(n_heads * width)[None, :]
    return (r == c // width).astype(BF16)


def _split3_dot(x, m):
    hi = x.astype(BF16)
    r1 = x - hi.astype(F32)
    mid = r1.astype(BF16)
    lo = (r1 - mid.astype(F32)).astype(BF16)
    return _dot(hi, m) + _dot(mid, m) + _dot(lo, m)


def _group_rms(y, seg, g, group):
    cols = []
    for c in range(y.shape[1] // group):
        yc = y[:, c * group:(c + 1) * group]
        ms = _split_dot(yc * yc, seg)
        cols.append(yc * lax.rsqrt(ms + NORM_EPS))
    return jnp.concatenate(cols, axis=1) * g


def _ssd_prompt_kernel(z_ref, x_ref, bc_ref, dt_ref, cw_ref, cb_ref, dtb_ref, alog_ref, dskip_ref, ng_ref,
                       tri_ref, e64_ref, e128_ref, seg_ref, y_ref, st_ref, carry_scr, yscr, *, chunk):
    c = pl.program_id(1)
    d_inner = x_ref.shape[1]
    gn = SSM_GROUPS * SSM_STATE

    @pl.when(c == 0)
    def _():
        carry_scr[...] = jnp.zeros_like(carry_scr)
        st_ref[...] = jnp.zeros_like(st_ref)

    def conv(raw, prev, w, b):
        ext = jnp.concatenate([prev, raw], axis=0)
        out = b + w[SSM_CONV - 1:SSM_CONV, :] * raw
        for k in range(1, SSM_CONV):
            shifted = pltpu.roll(ext, k, axis=0)[8:, :]
            out = out + w[SSM_CONV - 1 - k:SSM_CONV - k, :] * shifted
        return _silu(out)

    x_raw = x_ref[...]
    bc_raw = bc_ref[...]
    cw = cw_ref[...]
    cb = cb_ref[...]
    xs = conv(x_raw, carry_scr[:, :d_inner], cw[:, :d_inner], cb[:, :d_inner])
    bcm = conv(bc_raw, carry_scr[:, d_inner:], cw[:, d_inner:], cb[:, d_inner:])
    carry_scr[:, :d_inner] = x_raw[chunk - 8:, :]
    carry_scr[:, d_inner:] = bc_raw[chunk - 8:, :]
    bm = bcm[:, :gn].astype(BF16)
    cm = bcm[:, gn:].astype(BF16)

    dt = _softplus(dt_ref[...] + dtb_ref[...])
    a = dt * (-jnp.exp(alog_ref[...]))
    tri = tri_ref[...]
    acs = _split_dot_left(tri, a)
    acs_t = acs.T
    acs_col = _split_dot(acs, e128_ref[...])
    acs_x = _split_dot(acs, e64_ref[...])
    dt_x = _split_dot(dt, e64_ref[...])
    xd = xs * dt_x
    last_x = acs_x[chunk - 1:chunk, :]
    xdw = (xd * jnp.exp(last_x - acs_x)).astype(BF16)
    xdb = xd.astype(BF16)
    ea_x = jnp.exp(acs_x)

    lrow = lax.broadcasted_iota(jnp.int32, (chunk, chunk), 0)
    scol = lax.broadcasted_iota(jnp.int32, (chunk, chunk), 1)
    tril = scol <= lrow
    lane = lax.broadcasted_iota(jnp.int32, (1, LANES), 1)
    first = lane < SSM_HEAD_DIM
    heads_per_group = SSM_HEADS // SSM_GROUPS

    for pair in range(SSM_HEADS // 2):
        g = (2 * pair) // heads_per_group
        bg = bm[:, g * SSM_STATE:(g + 1) * SSM_STATE]
        cg = cm[:, g * SSM_STATE:(g + 1) * SSM_STATE]
        cbm = _dot_nt(cg, bg)
        xd_pair = xdb[:, pair * LANES:(pair + 1) * LANES]
        ys = []
        for e in range(2):
            hd = 2 * pair + e
            seg = acs_col[:, hd * LANES:(hd + 1) * LANES] - acs_t[hd:hd + 1, :]
            decay = jnp.exp(jnp.where(tril, seg, -jnp.inf))
            ys.append(_dot((cbm * decay).astype(BF16), xd_pair))
        y_diag = jnp.where(first, ys[0], ys[1])
        st_pair = st_ref[2 * pair:2 * pair + 2].reshape(2 * SSM_HEAD_DIM, SSM_STATE)
        y_off = _dot_nt(cg, st_pair.astype(BF16)) * ea_x[:, pair * LANES:(pair + 1) * LANES]
        yscr[:, pair * LANES:(pair + 1) * LANES] = y_diag + y_off
        new = _dot_tn(xdw[:, pair * LANES:(pair + 1) * LANES], bg)
        for e in range(2):
            hd = 2 * pair + e
            dec = jnp.exp(acs_col[chunk - 1:chunk, hd * LANES:(hd + 1) * LANES])
            st_ref[hd] = st_ref[hd] * dec + new[e * SSM_HEAD_DIM:(e + 1) * SSM_HEAD_DIM, :]

    y = yscr[...] + xs * dskip_ref[...]
    y = y * _silu(z_ref[...])
    y_ref[...] = _group_rms(y, seg_ref[...], ng_ref[...], d_inner // SSM_GROUPS).astype(y_ref.dtype)


def _ssd_consts(d_inner):
    r = jnp.arange(SSM_CHUNK)
    tri = (r[:, None] >= r[None, :]).astype(BF16)
    return (tri, _expand_matrix(LANES, SSM_HEADS, SSM_HEAD_DIM), _expand_matrix(LANES, SSM_HEADS, LANES),
            _seg_mean_matrix(d_inner // SSM_GROUPS, d_inner // SSM_GROUPS))


def _pad_lanes(v):
    return jnp.pad(v, (0, LANES - v.shape[0])).reshape(1, LANES)


def _ssd_prompt(zx, conv_w, conv_b, dt_bias, a_log, d_skip, norm_g, batch, seq):
    d_inner = SSM_HEADS * SSM_HEAD_DIM
    chunk = SSM_CHUNK
    nc = seq // chunk
    conv_dim = conv_w.shape[1]
    tri, e64, e128, seg = _ssd_consts(d_inner)
    dt_blk = (2 * d_inner + 2 * SSM_GROUPS * SSM_STATE) // LANES
    const = lambda b, c: (0, 0)
    y, st = pl.pallas_call(
        functools.partial(_ssd_prompt_kernel, chunk=chunk),
        grid=(batch, nc),
        in_specs=[
            pl.BlockSpec((chunk, d_inner), lambda b, c: (b * nc + c, 0)),
            pl.BlockSpec((chunk, d_inner), lambda b, c: (b * nc + c, 1)),
            pl.BlockSpec((chunk, d_inner), lambda b, c: (b * nc + c, 2)),
            pl.BlockSpec((chunk, LANES), lambda b, c: (b * nc + c, dt_blk)),
            pl.BlockSpec((SSM_CONV, conv_dim), const),
            pl.BlockSpec((1, conv_dim), const),
            pl.BlockSpec((1, LANES), const),
            pl.BlockSpec((1, LANES), const),
            pl.BlockSpec((1, d_inner), const),
            pl.BlockSpec((1, d_inner), const),
            pl.BlockSpec(tri.shape, const),
            pl.BlockSpec(e64.shape, const),
            pl.BlockSpec(e128.shape, const),
            pl.BlockSpec(seg.shape, const),
        ],
        out_specs=[
            pl.BlockSpec((chunk, d_inner), lambda b, c: (b * nc + c, 0)),
            pl.BlockSpec((None, SSM_HEADS, SSM_HEAD_DIM, SSM_STATE), lambda b, c: (b, 0, 0, 0)),
        ],
        out_shape=[jax.ShapeDtypeStruct((batch * seq, d_inner), BF16),
                   jax.ShapeDtypeStruct((batch, SSM_HEADS, SSM_HEAD_DIM, SSM_STATE), F32)],
        scratch_shapes=[pltpu.VMEM((8, conv_dim), F32), pltpu.VMEM((chunk, d_inner), F32)],
        compiler_params=_params("parallel", "arbitrary"),
        name="ssd_prompt",
    )(zx, zx, zx, zx, conv_w, conv_b.reshape(1, -1), _pad_lanes(dt_bias), _pad_lanes(a_log),
      jnp.repeat(d_skip, SSM_HEAD_DIM).reshape(1, d_inner), norm_g.reshape(1, d_inner), tri, e64, e128, seg)
    return y, st


def _ssd_decode_kernel(z_ref, x_ref, bc_ref, dt_ref, cs_ref, st_ref, cw_ref, cb_ref, dtb_ref, alog_ref,
                       dskip_ref, ng_ref, e64_ref, e128_ref, seg_ref, y_ref, so_ref, *, rb):
    d_inner = x_ref.shape[1]
    gn = SSM_GROUPS * SSM_STATE
    cw = cw_ref[...]
    xbc_raw = jnp.concatenate([x_ref[...], bc_ref[...]], axis=1)
    out = cb_ref[...] + cw[SSM_CONV - 1:SSM_CONV, :] * xbc_raw
    for j in range(SSM_CONV - 1):
        out = out + cw[j:j + 1, :] * cs_ref[j]
    xbc = _silu(out)
    xs = xbc[:, :d_inner]
    bm = xbc[:, d_inner:d_inner + gn]
    cm = xbc[:, d_inner + gn:]

    dt = _softplus(dt_ref[...] + dtb_ref[...])
    a = dt * (-jnp.exp(alog_ref[...]))
    a_col = _split3_dot(a, e128_ref[...])
    dt_x = _split3_dot(dt, e64_ref[...])
    xd = (xs * dt_x).astype(BF16)
    bmb = bm.astype(BF16)
    cmb = cm.astype(BF16)
    rows = lax.broadcasted_iota(jnp.int32, (rb, 1), 0)
    hpg = SSM_HEADS // SSM_GROUPS
    gw = hpg * SSM_HEAD_DIM

    ycols = []
    for g in range(SSM_GROUPS):
        bg = bmb[:, g * SSM_STATE:(g + 1) * SSM_STATE]
        cg = cmb[:, g * SSM_STATE:(g + 1) * SSM_STATE]
        xg = xd[:, g * gw:(g + 1) * gw]
        yg = jnp.zeros((rb, gw), F32)
        for r in range(rb):
            outer = _dot_tn(jnp.where(rows == r, xg, jnp.zeros_like(xg)), bg)
            news = []
            for e in range(hpg):
                hd = g * hpg + e
                dec = jnp.exp(a_col[r:r + 1, hd * LANES:(hd + 1) * LANES])
                new = st_ref[r, hd] * dec + outer[e * SSM_HEAD_DIM:(e + 1) * SSM_HEAD_DIM, :]
                so_ref[r, hd] = new
                news.append(new)
            yr = _dot_nt(cg, jnp.concatenate(news, axis=0).astype(BF16))
            yg = jnp.where(rows == r, yr, yg)
        ycols.append(yg)
    y = jnp.concatenate(ycols, axis=1) + xs * dskip_ref[...]
    y = y * _silu(z_ref[...])
    y_ref[...] = _group_rms(y, seg_ref[...], ng_ref[...], d_inner // SSM_GROUPS).astype(y_ref.dtype)


def _ssd_decode(zx, conv_state, ssm_state, conv_w, conv_b, dt_bias, a_log, d_skip, norm_g):
    r = zx.shape[0]
    d_inner = SSM_HEADS * SSM_HEAD_DIM
    rb = 8
    conv_dim = conv_w.shape[1]
    _, e64, e128, seg = _ssd_consts(d_inner)
    dt_blk = (2 * d_inner + 2 * SSM_GROUPS * SSM_STATE) // LANES
    cs = jnp.transpose(conv_state, (1, 0, 2))
    const = lambda i: (0, 0)
    st_spec = pl.BlockSpec((rb, SSM_HEADS, SSM_HEAD_DIM, SSM_STATE), lambda i: (i, 0, 0, 0))
    y, st = pl.pallas_call(
        functools.partial(_ssd_decode_kernel, rb=rb),
        grid=(r // rb,),
        in_specs=[
            pl.BlockSpec((rb, d_inner), lambda i: (i, 0)),
            pl.BlockSpec((rb, d_inner), lambda i: (i, 1)),
            pl.BlockSpec((rb, d_inner), lambda i: (i, 2)),
            pl.BlockSpec((rb, LANES), lambda i: (i, dt_blk)),
            pl.BlockSpec((SSM_CONV - 1, rb, conv_dim), lambda i: (0, i, 0)),
            st_spec,
            pl.BlockSpec((SSM_CONV, conv_dim), const),
            pl.BlockSpec((1, conv_dim), const),
            pl.BlockSpec((1, LANES), const),
            pl.BlockSpec((1, LANES), const),
            pl.BlockSpec((1, d_inner), const),
            pl.BlockSpec((1, d_inner), const),
            pl.BlockSpec(e64.shape, const),
            pl.BlockSpec(e128.shape, const),
            pl.BlockSpec(seg.shape, const),
        ],
        out_specs=[pl.BlockSpec((rb, d_inner), lambda i: (i, 0)), st_spec],
        out_shape=[jax.ShapeDtypeStruct((r, d_inner), BF16),
                   jax.ShapeDtypeStruct(ssm_state.shape, F32)],
        compiler_params=_params("parallel"),
        name="ssd_decode",
    )(zx, zx, zx, zx, cs, ssm_state, conv_w, conv_b.reshape(1, -1), _pad_lanes(dt_bias), _pad_lanes(a_log),
      jnp.repeat(d_skip, SSM_HEAD_DIM).reshape(1, d_inner), norm_g.reshape(1, d_inner), e64, e128, seg)
    return y, st


def _token_major(xt, heads):
    b, _, t = xt.shape
    n = len(heads)
    return jnp.transpose(xt.reshape((b,) + heads + (t,)), (0, n + 1) + tuple(range(1, n + 1)))


def kernel(x_prompt, x_sample, cache_sb_k, cache_sb_v, cache_diff_k, cache_diff_v, state_ssm_conv, state_ssm,
           page_table, ffn_norm, ffn_w_gate, ffn_w_up, ffn_w_down, mix_norm, sb_w_qkv, sb_w_o,
           diff_w_qkv, diff_q_norm, diff_k_norm, diff_lambda_q1, diff_lambda_k1, diff_lambda_q2, diff_lambda_k2,
           diff_subln, diff_w_o, ssm_w_in, ssm_conv_w, ssm_conv_b, ssm_dt_bias, ssm_a_log, ssm_d, ssm_norm,
           ssm_w_out):
    bp, seq, d = x_prompt.shape
    bs = x_sample.shape[0]
    depth = ffn_norm.shape[0]
    xp = x_prompt.reshape(bp * seq, d)
    xs = x_sample.reshape(bs, d)
    wg, wu, wd = ffn_w_gate.astype(BF16), ffn_w_up.astype(BF16), ffn_w_down.astype(BF16)

    sb_kp, sb_vp, sb_ks, sb_vs = [], [], [], []
    d_kp, d_vp, d_ks, d_vs = [], [], [], []
    cv_p, ss_p, cv_s, ss_s = [], [], [], []
    for i in range(depth):
        kind, j = i % N_MIXERS, i // N_MIXERS
        xp = _ffn(xp, ffn_norm[i, 0], wg[i, 0], wu[i, 0], wd[i, 0])
        xs = _ffn(xs, ffn_norm[i, 0], wg[i, 0], wu[i, 0], wd[i, 0])
        if kind == 0:
            w = sb_w_qkv[j].astype(BF16)
            wo = sb_w_o[j].astype(BF16)
            heads = (SB_HEADS, SB_HEAD_DIM)
            q, k, v, kt, vt = _qkv(xp, mix_norm[i], w, bp, 2)
            sb_kp.append(_token_major(kt, heads))
            sb_vp.append(_token_major(vt, heads))
            xp = _oproj(_sb_prompt(q, k, v, bp, seq), wo, xp)
            q, k, v, kt, vt = _qkv(xs, mix_norm[i], w, 1, 2)
            sb_ks.append(_token_major(kt, heads)[0][:, None])
            sb_vs.append(_token_major(vt, heads)[0][:, None])
            xs = _oproj(_sb_decode(q, cache_sb_k, cache_sb_v, j, page_table), wo, xs)
        elif kind == 1:
            lam_init = 0.8 - 0.6 * math.exp(-0.3 * i)
            w = diff_w_qkv[j].astype(BF16)
            wo = diff_w_o[j].astype(BF16)
            qg = jnp.tile(diff_q_norm[j], d // DIFF_HEAD_DIM)
            kg = jnp.tile(diff_k_norm[j], d // DIFF_HEAD_DIM)
            lam_params = jnp.stack([diff_lambda_q1[j], diff_lambda_k1[j], diff_lambda_q2[j], diff_lambda_k2[j]])
            heads = (2, DIFF_HEADS, DIFF_HEAD_DIM)
            q, k, v, kt = _qkv(xp, mix_norm[i], w, bp, 1, qg, kg)
            d_kp.append(_token_major(kt, heads))
            d_vp.append(v.reshape(bp, seq, DIFF_HEADS, 2 * DIFF_HEAD_DIM))
            o = _diff_prompt(q, k, v, lam_params, diff_subln[j], lam_init, bp, seq)
            xp = _oproj(o, wo, xp)
            q, k, v, kt = _qkv(xs, mix_norm[i], w, 1, 1, qg, kg)
            d_ks.append(_token_major(kt, heads)[0][:, None])
            d_vs.append(v.reshape(bs, 1, DIFF_HEADS, 2 * DIFF_HEAD_DIM))
            o = _diff_decode(q, k, v, cache_diff_k, cache_diff_v, j, page_table, lam_params, diff_subln[j],
                             lam_init)
            xs = _oproj(o, wo, xs)
        else:
            d_inner = SSM_HEADS * SSM_HEAD_DIM
            conv_dim = ssm_conv_w.shape[2]
            w_in = jnp.pad(ssm_w_in[j], ((0, 0), (0, LANES - SSM_HEADS))).astype(BF16)
            wo = ssm_w_out[j].astype(BF16)
            sp = (ssm_conv_w[j], ssm_conv_b[j], ssm_dt_bias[j], ssm_a_log[j], ssm_d[j], ssm_norm[j])
            tn = _tile(w_in.shape[1], 896)
            zx = _norm_matmul(xp, mix_norm[i], w_in, tn)
            y, st = _ssd_prompt(zx, *sp, bp, seq)
            cv_p.append(zx.reshape(bp, seq, -1)[:, seq - (SSM_CONV - 1):, d_inner:d_inner + conv_dim])
            ss_p.append(st)
            xp = _oproj(y, wo, xp)
            zx = _norm_matmul(xs, mix_norm[i], w_in, tn)
            y, st = _ssd_decode(zx, state_ssm_conv[j], state_ssm[j], *sp)
            cv_s.append(jnp.concatenate([state_ssm_conv[j][:, 1:], zx[:, None, d_inner:d_inner + conv_dim]], axis=1))
            ss_s.append(st)
            xs = _oproj(y, wo, xs)
        xp = _ffn(xp, ffn_norm[i, 1], wg[i, 1], wu[i, 1], wd[i, 1])
        xs = _ffn(xs, ffn_norm[i, 1], wg[i, 1], wu[i, 1], wd[i, 1])
    return (xp.reshape(bp, seq, d), xs.reshape(bs, 1, d),
            jnp.stack(sb_kp), jnp.stack(sb_vp), jnp.stack(sb_ks), jnp.stack(sb_vs),
            jnp.stack(d_kp), jnp.stack(d_vp), jnp.stack(d_ks), jnp.stack(d_vs),
            jnp.stack(cv_p), jnp.stack(ss_p), jnp.stack(cv_s), jnp.stack(ss_s))
```

```python
import functools
import math

import jax
import jax.numpy as jnp
from jax import lax
from jax.experimental import pallas as pl
from jax.experimental.pallas import tpu as pltpu

F32 = jnp.float32
BF16 = jnp.bfloat16

NORM_EPS = 1e-6
N_MIXERS = 3
SB_HEADS = 16
SB_HEAD_DIM = 64
DIFF_HEADS = 8
DIFF_HEAD_DIM = 64
SSM_HEAD_DIM = 64
SSM_HEADS = 32
SSM_GROUPS = 8
SSM_STATE = 128
SSM_CONV = 4
SSM_CHUNK = 128
LANES = 128
VMEM_LIMIT = 48 * 1024 * 1024


def _params(*sem):
    return pltpu.CompilerParams(dimension_semantics=sem, vmem_limit_bytes=VMEM_LIMIT)


def _tile(n, pref):
    t = min(n, pref)
    while n % t:
        t //= 2
    return t


def _dot(a, b):
    return jnp.dot(a, b, preferred_element_type=F32)


def _dot_nt(a, b):
    return lax.dot_general(a, b, (((1,), (1,)), ((), ())), preferred_element_type=F32)


def _dot_tn(a, b):
    return lax.dot_general(a, b, (((0,), (0,)), ((), ())), preferred_element_type=F32)


def _split_dot(x, m):
    hi = x.astype(BF16)
    lo = (x - hi.astype(F32)).astype(BF16)
    return _dot(hi, m) + _dot(lo, m)


def _split_dot_left(m, x):
    hi = x.astype(BF16)
    lo = (x - hi.astype(F32)).astype(BF16)
    return _dot(m, hi) + _dot(m, lo)


def _rms_rows(x, g):
    return x * lax.rsqrt(jnp.mean(x * x, axis=-1, keepdims=True) + NORM_EPS) * g


def _silu(x):
    return x / (1.0 + jnp.exp(-x))


def _softplus(z):
    return jnp.maximum(z, 0.0) + jnp.log(1.0 + jnp.exp(-jnp.abs(z)))


FFN_CHUNK = 512


def _ffn_kernel(x_ref, g_ref, wg_ref, wu_ref, wd_ref, o_ref, acc_scr):
    x = x_ref[...]
    h = _rms_rows(x, g_ref[...]).astype(BF16)
    dff = wg_ref.shape[1]
    for n, lo in enumerate(range(0, dff, FFN_CHUNK)):
        hi = min(lo + FFN_CHUNK, dff)
        gate = _dot(h, wg_ref[:, lo:hi])
        up = _dot(h, wu_ref[:, lo:hi])
        down = _dot((_silu(gate) * up).astype(BF16), wd_ref[lo:hi, :])
        if n == 0:
            acc_scr[...] = down
        else:
            acc_scr[...] += down
    o_ref[...] = x + 0.5 * acc_scr[...]


def _ffn(x, g, wg, wu, wd, layer, side):
    m, d = x.shape
    dff = wg.shape[3]
    tm = _tile(m, 1024)
    resident = dict(pipeline_mode=pl.Buffered(1))
    pick = lambda i: (layer, side, 0, 0)
    return pl.pallas_call(
        _ffn_kernel,
        grid=(m // tm,),
        in_specs=[
            pl.BlockSpec((tm, d), lambda i: (i, 0)),
            pl.BlockSpec((1, d), lambda i: (0, 0)),
            pl.BlockSpec((None, None, d, dff), pick, **resident),
            pl.BlockSpec((None, None, d, dff), pick, **resident),
            pl.BlockSpec((None, None, dff, d), pick, **resident),
        ],
        out_specs=pl.BlockSpec((tm, d), lambda i: (i, 0)),
        out_shape=jax.ShapeDtypeStruct((m, d), F32),
        scratch_shapes=[pltpu.VMEM((tm, d), F32)],
        compiler_params=_params("parallel"),
        name="ffn",
    )(x, g.reshape(1, d), wg, wu, wd)


def _norm_matmul_kernel(x_ref, g_ref, w_ref, o_ref, h_scr):
    @pl.when(pl.program_id(1) == 0)
    def _():
        h_scr[...] = _rms_rows(x_ref[...], g_ref[...]).astype(BF16)

    o_ref[...] = _dot(h_scr[...], w_ref[...])


def _norm_matmul(x, g, w, tn):
    m, d = x.shape
    n = w.shape[1]
    tm = _tile(m, 1024)
    return pl.pallas_call(
        _norm_matmul_kernel,
        grid=(m // tm, n // tn),
        in_specs=[
            pl.BlockSpec((tm, d), lambda i, j: (i, 0)),
            pl.BlockSpec((1, d), lambda i, j: (0, 0)),
            pl.BlockSpec((d, tn), lambda i, j: (0, j)),
        ],
        out_specs=pl.BlockSpec((tm, tn), lambda i, j: (i, j)),
        out_shape=jax.ShapeDtypeStruct((m, n), F32),
        scratch_shapes=[pltpu.VMEM((tm, d), BF16)],
        compiler_params=_params("parallel", "arbitrary"),
        name="norm_matmul",
    )(x, g.reshape(1, d), w)


def _head_rms(x, seg, g):
    cols = []
    for c in range(x.shape[1] // LANES):
        xc = x[:, c * LANES:(c + 1) * LANES]
        ms = _split_dot(xc * xc, seg)
        cols.append(xc * lax.rsqrt(ms + NORM_EPS))
    return jnp.concatenate(cols, axis=1) * g


def _qkv_kernel(x_ref, g_ref, wq_ref, wk_ref, wv_ref, seg_ref, qg_ref, kg_ref, *refs, qk_norm, n_prev):
    q_ref, k_ref, v_ref = refs[n_prev:n_prev + 3]
    t_refs = refs[n_prev + 3:]
    h = _rms_rows(x_ref[...], g_ref[...]).astype(BF16)
    q = _dot(h, wq_ref[...])
    k = _dot(h, wk_ref[...])
    if qk_norm:
        q = _head_rms(q, seg_ref[...], qg_ref[...])
        k = _head_rms(k, seg_ref[...], kg_ref[...])
    v = _dot(h, wv_ref[...])
    q_ref[...] = q
    k_ref[...] = k
    v_ref[...] = v
    for t_ref, val in zip(t_refs, (k, v)):
        t_ref[...] = val.T


def _seg_mean_matrix(n, group):
    r = jnp.arange(n)
    return jnp.where((r[:, None] // group) == (r[None, :] // group), 1.0 / group, 0.0).astype(BF16)


def _qkv(x, g, w, batch, n_transposed, qg=None, kg=None, layer=0, n_layers=1, stacks=()):
    m, d = x.shape
    seq = m // batch
    tm = _tile(seq, 512)
    nt = seq // tm
    qk_norm = qg is not None
    if not qk_norm:
        qg = kg = jnp.ones((d,), F32)
    row = lambda i: (i, 0)
    const = lambda i: (0, 0)
    out = jax.ShapeDtypeStruct((m, d), F32)
    t_spec = pl.BlockSpec((None, None, d, tm), lambda i: (layer, i // nt, 0, i % nt))
    t_out = jax.ShapeDtypeStruct((n_layers, batch, d, seq), F32)
    n_in = 8
    return pl.pallas_call(
        functools.partial(_qkv_kernel, qk_norm=qk_norm, n_prev=len(stacks)),
        grid=(m // tm,),
        in_specs=[
            pl.BlockSpec((tm, d), row),
            pl.BlockSpec((1, d), const),
            pl.BlockSpec((d, d), lambda i: (0, 0)),
            pl.BlockSpec((d, d), lambda i: (0, 1)),
            pl.BlockSpec((d, d), lambda i: (0, 2)),
            pl.BlockSpec((LANES, LANES), const),
            pl.BlockSpec((1, d), const),
            pl.BlockSpec((1, d), const),
        ] + [pl.BlockSpec(memory_space=pl.ANY)] * len(stacks),
        out_specs=[pl.BlockSpec((tm, d), row)] * 3 + [t_spec] * n_transposed,
        out_shape=[out, out, out] + [t_out] * n_transposed,
        input_output_aliases={n_in + p: 3 + p for p in range(len(stacks))},
        compiler_params=_params("parallel"),
        name="qkv_norm" if qk_norm else "qkv",
    )(x, g.reshape(1, d), w, w, w, _seg_mean_matrix(LANES, DIFF_HEAD_DIM),
      qg.reshape(1, d), kg.reshape(1, d), *stacks)


def _oproj_kernel(o_ref, w_ref, r_ref, y_ref):
    y_ref[...] = r_ref[...] + _dot(o_ref[...].astype(BF16), w_ref[...])


def _oproj(o, w, res):
    m, k = o.shape
    d = w.shape[1]
    tm = _tile(m, 512)
    return pl.pallas_call(
        _oproj_kernel,
        grid=(m // tm,),
        in_specs=[
            pl.BlockSpec((tm, k), lambda i: (i, 0)),
            pl.BlockSpec((k, d), lambda i: (0, 0)),
            pl.BlockSpec((tm, d), lambda i: (i, 0)),
        ],
        out_specs=pl.BlockSpec((tm, d), lambda i: (i, 0)),
        out_shape=jax.ShapeDtypeStruct((m, d), F32),
        compiler_params=_params("parallel"),
        name="oproj",
    )(o, w, res)


def _suffix_matrix(n):
    r = jnp.arange(n)
    return (r[:, None] > r[None, :]).astype(BF16)


def _sb_prompt_kernel(q_ref, k_ref, v_ref, u_ref, o_ref, kb_scr, vb_scr, acc_scr, *, tk, halves):
    qi = pl.program_id(2)

    @pl.when(qi == 0)
    def _():
        kb_scr[...] = k_ref[...].astype(BF16)
        vb_scr[...] = v_ref[...].astype(BF16)

    q2 = q_ref[...] * (SB_HEAD_DIM ** -0.5)
    lane = lax.broadcasted_iota(jnp.int32, (1, LANES), 1)
    row = lax.broadcasted_iota(jnp.int32, (tk, tk), 0)
    col = lax.broadcasted_iota(jnp.int32, (tk, tk), 1)
    causal = col < row
    u = u_ref[...]
    base = qi * halves
    chains = [(hh, a) for hh in range(2) for a in range(halves)]
    qms = []
    for hh, a in chains:
        in_head = (lane // SB_HEAD_DIM) == hh
        qms.append(jnp.where(in_head, q2[a * tk:(a + 1) * tk], 0.0).astype(BF16))

    def tiles(cs, kjs, rs, masked):
        starts = [pl.multiple_of(kj * tk, tk) for kj in kjs]
        zs = [_dot_nt(qms[c], kb_scr[pl.ds(st, tk), :]) for c, st in zip(cs, starts)]
        sps = [_softplus(z) for z in zs]
        if masked:
            sps = [jnp.where(causal, sp, 0.0) for sp in sps]
        cums = [_dot(sp.astype(BF16), u) for sp in sps]
        ws = [jnp.exp(z - sp - cum - r) for z, sp, cum, r in zip(zs, sps, cums, rs)]
        if masked:
            ws = [jnp.where(causal, w, 0.0) for w in ws]
        for c, st, w in zip(cs, starts, ws):
            pv = _dot(w.astype(BF16), vb_scr[pl.ds(st, tk), :])
            if masked:
                acc_scr[c] = pv
            else:
                acc_scr[c] += pv
        return [r + cum[:, :1] + sp[:, :1] for r, cum, sp in zip(rs, cums, sps)]

    n_chain = len(chains)
    rs = [jnp.zeros((tk, 1), F32)] * n_chain
    for t in range(halves):
        cs = [c for c in range(n_chain) if chains[c][1] >= t]
        new = tiles(cs, [base + chains[c][1] - t for c in cs], [rs[c] for c in cs], t == 0)
        for c, r in zip(cs, new):
            rs[c] = r

    depth = 2 if halves % 2 == 0 else 1

    def body(j, rs):
        kjs = [base - 1 - depth * j - t for t in range(depth)]
        starts = [pl.multiple_of(kj * tk, tk) for kj in kjs]
        zs = [[_dot_nt(qms[c], kb_scr[pl.ds(st, tk), :]) for c in range(n_chain)] for st in starts]
        sps = [[_softplus(z) for z in zt] for zt in zs]
        cums = [[_dot(sp.astype(BF16), u) for sp in spt] for spt in sps]
        rs = list(rs)
        for c in range(n_chain):
            pv = None
            for t in range(depth):
                w = jnp.exp(zs[t][c] - sps[t][c] - cums[t][c] - rs[c]).astype(BF16)
                d = _dot(w, vb_scr[pl.ds(starts[t], tk), :])
                pv = d if pv is None else pv + d
                rs[c] = rs[c] + cums[t][c][:, :1] + sps[t][c][:, :1]
            acc_scr[c] += pv
        return tuple(rs)

    lax.fori_loop(0, base // depth, body, tuple(rs))
    for a in range(halves):
        o_ref[a * tk:(a + 1) * tk, :] = jnp.where(
            lane < SB_HEAD_DIM, acc_scr[a], acc_scr[halves + a]).astype(o_ref.dtype)


ATTN_TK = 256
ATTN_HALVES = 2


def _attn_tiles(seq):
    tk = _tile(seq, ATTN_TK)
    halves = ATTN_HALVES if seq % (ATTN_HALVES * tk) == 0 else 1
    return tk, halves


def _sb_prompt(q, k, v, batch, seq):
    m, d = q.shape
    tk, halves = _attn_tiles(seq)
    tq = tk * halves
    nq = seq // tq
    pairs = d // LANES
    k3 = k.reshape(batch, seq, d)
    v3 = v.reshape(batch, seq, d)
    return pl.pallas_call(
        functools.partial(_sb_prompt_kernel, tk=tk, halves=halves),
        grid=(batch, pairs, nq),
        in_specs=[
            pl.BlockSpec((tq, LANES), lambda b, p, i: (b * nq + i, p)),
            pl.BlockSpec((None, seq, LANES), lambda b, p, i: (b, 0, p)),
            pl.BlockSpec((None, seq, LANES), lambda b, p, i: (b, 0, p)),
            pl.BlockSpec((tk, tk), lambda b, p, i: (0, 0)),
        ],
        out_specs=pl.BlockSpec((tq, LANES), lambda b, p, i: (b * nq + i, p)),
        out_shape=jax.ShapeDtypeStruct((m, d), BF16),
        scratch_shapes=[pltpu.VMEM((seq, LANES), BF16), pltpu.VMEM((seq, LANES), BF16),
                        pltpu.VMEM((2 * halves, tk, LANES), F32)],
        compiler_params=_params("parallel", "parallel", "arbitrary"),
        name="sb_prompt",
    )(q, k3, v3, _suffix_matrix(tk))


DECODE_PAGES_PER_STEP = 8


def _pages_per_step(n_pages):
    pps = DECODE_PAGES_PER_STEP
    while n_pages % pps:
        pps //= 2
    return pps


def _slot_minor(cache):
    nd = cache.ndim
    t = jnp.transpose(cache, (0, 1) + tuple(range(3, nd)) + (2,))
    return t.reshape(cache.shape[0], cache.shape[1], -1, cache.shape[2])


def _head_rows(x_row, n_rows):
    width = x_row.shape[1] // n_rows
    r = lax.broadcasted_iota(jnp.int32, (n_rows, n_rows * width), 0)
    c = lax.broadcasted_iota(jnp.int32, (n_rows, n_rows * width), 1)
    own = (c // width) == r
    return jnp.where(own, x_row, 0.0), own


def _sb_decode_kernel(pt_ref, q_ref, *refs, pps):
    k_refs, v_refs = refs[:pps], refs[pps:2 * pps]
    u_ref, o_ref, q_scr, acc_scr, r_scr = refs[2 * pps:]
    p = pl.program_id(1)

    @pl.when(p == 0)
    def _():
        qrows, _ = _head_rows(q_ref[0] * (SB_HEAD_DIM ** -0.5), SB_HEADS)
        q_scr[...] = qrows.astype(BF16)
        acc_scr[...] = jnp.zeros_like(acc_scr)
        r_scr[...] = jnp.zeros_like(r_scr)

    q = q_scr[...]
    r = r_scr[...]
    acc = acc_scr[...]
    zts = [_dot(q, k_ref[...].astype(BF16)) for k_ref in k_refs]
    sps = [_softplus(zt) for zt in zts]
    cums = [_split_dot(sp, u_ref[...]) for sp in sps]
    for zt, sp, cum, v_ref in zip(zts, sps, cums, v_refs):
        wb = jnp.exp(zt - sp - cum - r).astype(BF16)
        acc = acc + _dot_nt(wb, v_ref[...].astype(BF16))
        r = r + cum[:, :1] + sp[:, :1]
    acc_scr[...] = acc
    r_scr[...] = r

    @pl.when(p == pl.num_programs(1) - 1)
    def _():
        _, own = _head_rows(jnp.zeros((1, acc.shape[1]), F32), SB_HEADS)
        o_ref[0] = jnp.sum(jnp.where(own, acc, 0.0), axis=0, keepdims=True)


def _sb_decode(q, cache_k, cache_v, layer, page_table):
    r, d = q.shape
    n_pages = page_table.shape[1]
    page = cache_k.shape[2]
    pps = _pages_per_step(n_pages)
    pt = page_table.reshape(-1)

    cache_k, cache_v = _slot_minor(cache_k), _slot_minor(cache_v)

    def kv_map(i):
        return lambda b, p, pt_ref: (layer, pt_ref[b * n_pages + n_pages - 1 - (p * pps + i)], 0, 0)

    kv_specs = [pl.BlockSpec((None, None, d, page), kv_map(i)) for i in range(pps)]
    row_map = lambda b, p, pt_ref: (b, 0, 0)
    out = pl.pallas_call(
        functools.partial(_sb_decode_kernel, pps=pps),
        grid_spec=pltpu.PrefetchScalarGridSpec(
            num_scalar_prefetch=1,
            grid=(r, n_pages // pps),
            in_specs=[pl.BlockSpec((1, 1, d), row_map)] + kv_specs + kv_specs
            + [pl.BlockSpec((page, page), lambda b, p, pt_ref: (0, 0))],
            out_specs=pl.BlockSpec((1, 1, d), row_map),
            scratch_shapes=[pltpu.VMEM((SB_HEADS, d), BF16), pltpu.VMEM((SB_HEADS, d), F32),
                            pltpu.VMEM((SB_HEADS, 1), F32)],
        ),
        out_shape=jax.ShapeDtypeStruct((r, 1, d), F32),
        compiler_params=_params("parallel", "arbitrary"),
        name="sb_decode",
    )(pt, q.reshape(r, 1, d), *([cache_k] * pps), *([cache_v] * pps), _suffix_matrix(page))
    return out.reshape(r, d)


def _lambda(lq1, lk1, lq2, lk2, lam_init):
    return (jnp.exp(jnp.sum(lq1 * lk1, axis=-1, keepdims=True))
            - jnp.exp(jnp.sum(lq2 * lk2, axis=-1, keepdims=True)) + lam_init)


def _diff_prompt_kernel(slope_ref, q0_ref, q1_ref, k0_ref, k1_ref, v_ref, lam_ref, sg_ref, o_ref,
                        k0_scr, k1_scr, vb_scr, acc_scr, *, tk, halves, lam_init):
    h = pl.program_id(1)
    qi = pl.program_id(2)

    @pl.when(qi == 0)
    def _():
        k0_scr[...] = k0_ref[...].astype(BF16)
        k1_scr[...] = k1_ref[...].astype(BF16)
        vb_scr[...] = v_ref[...].astype(BF16)

    slope = slope_ref[h]
    lane = lax.broadcasted_iota(jnp.int32, (1, LANES), 1)
    in_head = (lane // DIFF_HEAD_DIM) == (h % 2)
    row = lax.broadcasted_iota(jnp.int32, (tk, tk), 0)
    col = lax.broadcasted_iota(jnp.int32, (tk, tk), 1)
    causal = col <= row
    rel = lax.broadcasted_iota(jnp.int32, (1, tk), 1).astype(F32)
    base = qi * halves
    k_scrs = (k0_scr, k1_scr)
    chains = [(c, a) for c in range(2) for a in range(halves)]
    qms = []
    for c, a in chains:
        q = (q0_ref, q1_ref)[c][a * tk:(a + 1) * tk, :]
        qms.append(jnp.where(in_head, q * (DIFF_HEAD_DIM ** -0.5), 0.0).astype(BF16))

    def tiles(cs, kjs, states, diag):
        starts = [pl.multiple_of(kj * tk, tk) for kj in kjs]
        ss = [_dot_nt(qms[ci], k_scrs[chains[ci][0]][pl.ds(st, tk), :])
              + slope * (rel + ((kj - base) * tk).astype(F32)) for ci, st, kj in zip(cs, starts, kjs)]
        if diag:
            ss = [jnp.where(causal, s, -jnp.inf) for s in ss]
            ms = [jnp.max(s, axis=-1, keepdims=True) for s in ss]
        else:
            ms = [jnp.maximum(m, jnp.max(s, axis=-1, keepdims=True)) for (m, _), s in zip(states, ss)]
        ps = [jnp.exp(s - m) for s, m in zip(ss, ms)]
        out = []
        for i, (ci, st, p) in enumerate(zip(cs, starts, ps)):
            pv = _dot(p.astype(BF16), vb_scr[pl.ds(st, tk), :])
            psum = jnp.sum(p, axis=-1, keepdims=True)
            if diag:
                acc_scr[ci] = pv
                out.append((ms[i], psum))
            else:
                alpha = jnp.exp(states[i][0] - ms[i])
                acc_scr[ci] = alpha * acc_scr[ci] + pv
                out.append((ms[i], alpha * states[i][1] + psum))
        return out

    n_chain = len(chains)
    states = [None] * n_chain
    for t in range(halves):
        cs = [ci for ci in range(n_chain) if chains[ci][1] >= t]
        new = tiles(cs, [base + chains[ci][1] - t for ci in cs], [states[ci] for ci in cs], t == 0)
        for ci, st in zip(cs, new):
            states[ci] = st

    def body(j, states):
        return tuple(tiles(list(range(n_chain)), [j] * n_chain, list(states), False))

    states = lax.fori_loop(0, base, body, tuple(states))
    lam = _lambda(lam_ref[0:1, :], lam_ref[1:2, :], lam_ref[2:3, :], lam_ref[3:4, :], lam_init)
    for a in range(halves):
        o = acc_scr[a] / states[a][1] - lam * (acc_scr[halves + a] / states[halves + a][1])
        o = o * lax.rsqrt(jnp.mean(o * o, axis=-1, keepdims=True) + NORM_EPS) * sg_ref[...]
        o_ref[a * tk:(a + 1) * tk, :] = (o * (1.0 - lam_init)).astype(o_ref.dtype)


def _alibi_slopes(n_heads):
    return jnp.exp2(-8.0 * jnp.arange(1, n_heads + 1, dtype=F32) / n_heads)


def _diff_prompt(q, k, v, lam_params, subln_g, lam_init, batch, seq):
    m, d = q.shape
    tk, halves = _attn_tiles(seq)
    tq = tk * halves
    nq = seq // tq
    half = d // (2 * LANES)
    k3 = k.reshape(batch, seq, d)
    v3 = v.reshape(batch, seq, d)
    return pl.pallas_call(
        functools.partial(_diff_prompt_kernel, tk=tk, halves=halves, lam_init=lam_init),
        grid_spec=pltpu.PrefetchScalarGridSpec(
            num_scalar_prefetch=0,
            grid=(batch, DIFF_HEADS, nq),
            in_specs=[
                pl.BlockSpec(memory_space=pltpu.SMEM),
                pl.BlockSpec((tq, LANES), lambda b, h, i: (b * nq + i, h // 2)),
                pl.BlockSpec((tq, LANES), lambda b, h, i: (b * nq + i, half + h // 2)),
                pl.BlockSpec((None, seq, LANES), lambda b, h, i: (b, 0, h // 2)),
                pl.BlockSpec((None, seq, LANES), lambda b, h, i: (b, 0, half + h // 2)),
                pl.BlockSpec((None, seq, LANES), lambda b, h, i: (b, 0, h)),
                pl.BlockSpec((4, DIFF_HEAD_DIM), lambda b, h, i: (0, 0)),
                pl.BlockSpec((1, LANES), lambda b, h, i: (0, 0)),
            ],
            out_specs=pl.BlockSpec((tq, LANES), lambda b, h, i: (b * nq + i, h)),
            scratch_shapes=[pltpu.VMEM((seq, LANES), BF16)] * 3 + [pltpu.VMEM((2 * halves, tk, LANES), F32)],
        ),
        out_shape=jax.ShapeDtypeStruct((m, d), BF16),
        compiler_params=_params("parallel", "parallel", "arbitrary"),
        name="diff_prompt",
    )(_alibi_slopes(DIFF_HEADS), q, q, k3, k3, v3, lam_params, subln_g.reshape(1, LANES))


def _diff_decode_kernel(pt_ref, q_ref, kn_ref, vn_ref, *refs, pps, lam_init, past):
    k_refs, v_refs = refs[:pps], refs[pps:2 * pps]
    slope_ref, lam_ref, sg_ref, rep_ref, o_ref, q_scr, acc_scr, m_scr, l_scr = refs[2 * pps:]
    p = pl.program_id(1)
    nrow = 2 * DIFF_HEADS
    page = k_refs[0].shape[1]

    @pl.when(p == 0)
    def _():
        qb = (q_ref[0] * (DIFF_HEAD_DIM ** -0.5)).astype(BF16).astype(F32)
        qrows, _ = _head_rows(qb, nrow)
        q_scr[...] = qrows.astype(BF16)
        kn = kn_ref[0].astype(BF16).astype(F32)
        m_scr[...] = jnp.sum(qrows * kn, axis=-1, keepdims=True)
        l_scr[...] = jnp.ones_like(l_scr)
        vn = vn_ref[0].astype(BF16).astype(F32)
        acc_scr[...] = jnp.concatenate([vn, vn], axis=0)

    q = q_scr[...]
    big = (nrow, page * DIFF_HEADS)
    own_head = (lax.broadcasted_iota(jnp.int32, big, 1) % DIFF_HEADS
                == lax.broadcasted_iota(jnp.int32, big, 0) % DIFF_HEADS)
    m, l, acc = m_scr[...], l_scr[...], acc_scr[...]
    ss = []
    for i, k_ref in enumerate(k_refs):
        pos = (p * pps + i) * page + lax.broadcasted_iota(jnp.int32, (1, page), 1)
        ss.append(_dot(q, k_ref[...].astype(BF16)) - slope_ref[...] * (past - pos).astype(F32))
    m_new = m
    for s in ss:
        m_new = jnp.maximum(m_new, jnp.max(s, axis=-1, keepdims=True))
    alpha = jnp.exp(m - m_new)
    prs = [jnp.exp(s - m_new) for s in ss]
    l = alpha * l
    acc = alpha * acc
    for pr, v_ref in zip(prs, v_refs):
        l = l + jnp.sum(pr, axis=-1, keepdims=True)
        spread = _dot(pr.astype(BF16), rep_ref[...])
        pbig = jnp.where(own_head, spread, 0.0).astype(BF16)
        acc = acc + _dot(pbig, v_ref[...].astype(BF16))
    m = m_new
    m_scr[...] = m
    l_scr[...] = l
    acc_scr[...] = acc

    @pl.when(p == pl.num_programs(1) - 1)
    def _():
        a = acc / l
        lam = _lambda(lam_ref[0:1, :], lam_ref[1:2, :], lam_ref[2:3, :], lam_ref[3:4, :], lam_init)
        dd = a[:DIFF_HEADS] - lam * a[DIFF_HEADS:]
        dn = dd * lax.rsqrt(jnp.mean(dd * dd, axis=-1, keepdims=True) + NORM_EPS)
        o_ref[0] = dn * sg_ref[...] * (1.0 - lam_init)


def _diff_decode(q, k_new, v_new, cache_k, cache_v, layer, page_table, lam_params, subln_g, lam_init):
    r, d = q.shape
    n_pages = page_table.shape[1]
    page = cache_k.shape[2]
    pps = _pages_per_step(n_pages)
    nrow = 2 * DIFF_HEADS
    vdim = 2 * DIFF_HEAD_DIM
    pt = page_table.reshape(-1)

    def page_of(i):
        return lambda b, p, pt_ref: pt_ref[b * n_pages + p * pps + i]

    cache_k = _slot_minor(cache_k)
    cache_v = cache_v.reshape(cache_v.shape[:2] + (page * DIFF_HEADS, vdim))
    rep = _expand_matrix(page, page, DIFF_HEADS)
    k_specs = [pl.BlockSpec((None, None, d, page),
                            lambda b, p, pt_ref, f=page_of(i): (layer, f(b, p, pt_ref), 0, 0))
               for i in range(pps)]
    v_specs = [pl.BlockSpec((None, None, page * DIFF_HEADS, vdim),
                            lambda b, p, pt_ref, f=page_of(i): (layer, f(b, p, pt_ref), 0, 0))
               for i in range(pps)]
    row_map = lambda b, p, pt_ref: (b, 0, 0)
    const = lambda b, p, pt_ref: (0, 0)
    slopes = jnp.tile(_alibi_slopes(DIFF_HEADS), 2).reshape(nrow, 1)
    out = pl.pallas_call(
        functools.partial(_diff_decode_kernel, pps=pps, lam_init=lam_init, past=n_pages * page),
        grid_spec=pltpu.PrefetchScalarGridSpec(
            num_scalar_prefetch=1,
            grid=(r, n_pages // pps),
            in_specs=[
                pl.BlockSpec((1, 1, d), row_map),
                pl.BlockSpec((1, 1, d), row_map),
                pl.BlockSpec((1, DIFF_HEADS, vdim), row_map),
            ] + k_specs + v_specs + [
                pl.BlockSpec((nrow, 1), const),
                pl.BlockSpec((4, DIFF_HEAD_DIM), const),
                pl.BlockSpec((1, vdim), const),
                pl.BlockSpec(rep.shape, const),
            ],
            out_specs=pl.BlockSpec((1, DIFF_HEADS, vdim), row_map),
            scratch_shapes=[pltpu.VMEM((nrow, d), BF16), pltpu.VMEM((nrow, vdim), F32),
                            pltpu.VMEM((nrow, 1), F32), pltpu.VMEM((nrow, 1), F32)],
        ),
        out_shape=jax.ShapeDtypeStruct((r, DIFF_HEADS, vdim), F32),
        compiler_params=_params("parallel", "arbitrary"),
        name="diff_decode",
    )(pt, q.reshape(r, 1, d), k_new.reshape(r, 1, d), v_new.reshape(r, DIFF_HEADS, vdim),
      *([cache_k] * pps), *([cache_v] * pps), slopes, lam_params, subln_g.reshape(1, vdim), rep)
    return out.reshape(r, d)


def _expand_matrix(n_in, n_heads, width):
    r = jnp.arange(n_in)[:, None]
    c = jnp.arange(n_heads * width)[None, :]
    return (r == c // width).astype(BF16)


def _split3_dot(x, m):
    hi = x.astype(BF16)
    r1 = x - hi.astype(F32)
    mid = r1.astype(BF16)
    lo = (r1 - mid.astype(F32)).astype(BF16)
    return _dot(hi, m) + _dot(mid, m) + _dot(lo, m)


def _group_rms(y, seg, g, group):
    cols = []
    for c in range(y.shape[1] // group):
        yc = y[:, c * group:(c + 1) * group]
        ms = _split_dot(yc * yc, seg)
        cols.append(yc * lax.rsqrt(ms + NORM_EPS))
    return jnp.concatenate(cols, axis=1) * g


def _ssd_prompt_kernel(z_ref, x_ref, bc_ref, dt_ref, cw_ref, cb_ref, dtb_ref, alog_ref, dskip_ref, ng_ref,
                       tri_ref, e64_ref, e128_ref, seg_ref, y_ref, st_ref, carry_scr, yscr, *, chunk):
    c = pl.program_id(1)
    d_inner = x_ref.shape[1]
    gn = SSM_GROUPS * SSM_STATE

    @pl.when(c == 0)
    def _():
        carry_scr[...] = jnp.zeros_like(carry_scr)
        st_ref[...] = jnp.zeros_like(st_ref)

    def conv(raw, prev, w, b):
        ext = jnp.concatenate([prev, raw], axis=0)
        out = b + w[SSM_CONV - 1:SSM_CONV, :] * raw
        for k in range(1, SSM_CONV):
            shifted = pltpu.roll(ext, k, axis=0)[8:, :]
            out = out + w[SSM_CONV - 1 - k:SSM_CONV - k, :] * shifted
        return _silu(out)

    x_raw = x_ref[...]
    bc_raw = bc_ref[...]
    cw = cw_ref[...]
    cb = cb_ref[...]
    xs = conv(x_raw, carry_scr[:, :d_inner], cw[:, :d_inner], cb[:, :d_inner])
    bcm = conv(bc_raw, carry_scr[:, d_inner:], cw[:, d_inner:], cb[:, d_inner:])
    carry_scr[:, :d_inner] = x_raw[chunk - 8:, :]
    carry_scr[:, d_inner:] = bc_raw[chunk - 8:, :]
    bm = bcm[:, :gn].astype(BF16)
    cm = bcm[:, gn:].astype(BF16)

    dt = _softplus(dt_ref[...] + dtb_ref[...])
    a = dt * (-jnp.exp(alog_ref[...]))
    tri = tri_ref[...]
    acs = _split_dot_left(tri, a)
    acs_t = acs.T
    acs_col = _split_dot(acs, e128_ref[...])
    acs_x = _split_dot(acs, e64_ref[...])
    dt_x = _split_dot(dt, e64_ref[...])
    xd = xs * dt_x
    last_x = acs_x[chunk - 1:chunk, :]
    xdw = (xd * jnp.exp(last_x - acs_x)).astype(BF16)
    xdb = xd.astype(BF16)
    ea_x = jnp.exp(acs_x)

    lrow = lax.broadcasted_iota(jnp.int32, (chunk, chunk), 0)
    scol = lax.broadcasted_iota(jnp.int32, (chunk, chunk), 1)
    tril = scol <= lrow
    lane = lax.broadcasted_iota(jnp.int32, (1, LANES), 1)
    first = lane < SSM_HEAD_DIM
    heads_per_group = SSM_HEADS // SSM_GROUPS

    for pair in range(SSM_HEADS // 2):
        g = (2 * pair) // heads_per_group
        bg = bm[:, g * SSM_STATE:(g + 1) * SSM_STATE]
        cg = cm[:, g * SSM_STATE:(g + 1) * SSM_STATE]
        cbm = _dot_nt(cg, bg)
        xd_pair = xdb[:, pair * LANES:(pair + 1) * LANES]
        ys = []
        for e in range(2):
            hd = 2 * pair + e
            seg = acs_col[:, hd * LANES:(hd + 1) * LANES] - acs_t[hd:hd + 1, :]
            decay = jnp.exp(jnp.where(tril, seg, -jnp.inf))
            ys.append(_dot((cbm * decay).astype(BF16), xd_pair))
        y_diag = jnp.where(first, ys[0], ys[1])
        st_pair = st_ref[2 * pair:2 * pair + 2].reshape(2 * SSM_HEAD_DIM, SSM_STATE)
        y_off = _dot_nt(cg, st_pair.astype(BF16)) * ea_x[:, pair * LANES:(pair + 1) * LANES]
        yscr[:, pair * LANES:(pair + 1) * LANES] = y_diag + y_off
        new = _dot_tn(xdw[:, pair * LANES:(pair + 1) * LANES], bg)
        for e in range(2):
            hd = 2 * pair + e
            dec = jnp.exp(acs_col[chunk - 1:chunk, hd * LANES:(hd + 1) * LANES])
            st_ref[hd] = st_ref[hd] * dec + new[e * SSM_HEAD_DIM:(e + 1) * SSM_HEAD_DIM, :]

    y = yscr[...] + xs * dskip_ref[...]
    y = y * _silu(z_ref[...])
    y_ref[...] = _group_rms(y, seg_ref[...], ng_ref[...], d_inner // SSM_GROUPS).astype(y_ref.dtype)


def _ssd_consts(d_inner):
    r = jnp.arange(SSM_CHUNK)
    tri = (r[:, None] >= r[None, :]).astype(BF16)
    return (tri, _expand_matrix(LANES, SSM_HEADS, SSM_HEAD_DIM), _expand_matrix(LANES, SSM_HEADS, LANES),
            _seg_mean_matrix(d_inner // SSM_GROUPS, d_inner // SSM_GROUPS))


def _pad_lanes(v):
    return jnp.pad(v, (0, LANES - v.shape[0])).reshape(1, LANES)


def _ssd_prompt(zx, conv_w, conv_b, dt_bias, a_log, d_skip, norm_g, batch, seq):
    d_inner = SSM_HEADS * SSM_HEAD_DIM
    chunk = SSM_CHUNK
    nc = seq // chunk
    conv_dim = conv_w.shape[1]
    tri, e64, e128, seg = _ssd_consts(d_inner)
    dt_blk = (2 * d_inner + 2 * SSM_GROUPS * SSM_STATE) // LANES
    const = lambda b, c: (0, 0)
    y, st = pl.pallas_call(
        functools.partial(_ssd_prompt_kernel, chunk=chunk),
        grid=(batch, nc),
        in_specs=[
            pl.BlockSpec((chunk, d_inner), lambda b, c: (b * nc + c, 0)),
            pl.BlockSpec((chunk, d_inner), lambda b, c: (b * nc + c, 1)),
            pl.BlockSpec((chunk, d_inner), lambda b, c: (b * nc + c, 2)),
            pl.BlockSpec((chunk, LANES), lambda b, c: (b * nc + c, dt_blk)),
            pl.BlockSpec((SSM_CONV, conv_dim), const),
            pl.BlockSpec((1, conv_dim), const),
            pl.BlockSpec((1, LANES), const),
            pl.BlockSpec((1, LANES), const),
            pl.BlockSpec((1, d_inner), const),
            pl.BlockSpec((1, d_inner), const),
            pl.BlockSpec(tri.shape, const),
            pl.BlockSpec(e64.shape, const),
            pl.BlockSpec(e128.shape, const),
            pl.BlockSpec(seg.shape, const),
        ],
        out_specs=[
            pl.BlockSpec((chunk, d_inner), lambda b, c: (b * nc + c, 0)),
            pl.BlockSpec((None, SSM_HEADS, SSM_HEAD_DIM, SSM_STATE), lambda b, c: (b, 0, 0, 0)),
        ],
        out_shape=[jax.ShapeDtypeStruct((batch * seq, d_inner), BF16),
                   jax.ShapeDtypeStruct((batch, SSM_HEADS, SSM_HEAD_DIM, SSM_STATE), F32)],
        scratch_shapes=[pltpu.VMEM((8, conv_dim), F32), pltpu.VMEM((chunk, d_inner), F32)],
        compiler_params=_params("parallel", "arbitrary"),
        name="ssd_prompt",
    )(zx, zx, zx, zx, conv_w, conv_b.reshape(1, -1), _pad_lanes(dt_bias), _pad_lanes(a_log),
      jnp.repeat(d_skip, SSM_HEAD_DIM).reshape(1, d_inner), norm_g.reshape(1, d_inner), tri, e64, e128, seg)
    return y, st


def _ssd_decode_kernel(z_ref, x_ref, bc_ref, dt_ref, cs_ref, st_ref, cw_ref, cb_ref, dtb_ref, alog_ref,
                       dskip_ref, ng_ref, e64_ref, e128_ref, seg_ref, y_ref, so_ref, *, rb):
    d_inner = x_ref.shape[1]
    gn = SSM_GROUPS * SSM_STATE
    cw = cw_ref[...]
    xbc_raw = jnp.concatenate([x_ref[...], bc_ref[...]], axis=1)
    out = cb_ref[...] + cw[SSM_CONV - 1:SSM_CONV, :] * xbc_raw
    for j in range(SSM_CONV - 1):
        out = out + cw[j:j + 1, :] * cs_ref[j]
    xbc = _silu(out)
    xs = xbc[:, :d_inner]
    bm = xbc[:, d_inner:d_inner + gn]
    cm = xbc[:, d_inner + gn:]

    dt = _softplus(dt_ref[...] + dtb_ref[...])
    a = dt * (-jnp.exp(alog_ref[...]))
    a_col = _split3_dot(a, e128_ref[...])
    dt_x = _split3_dot(dt, e64_ref[...])
    xd = (xs * dt_x).astype(BF16)
    bmb = bm.astype(BF16)
    cmb = cm.astype(BF16)
    rows = lax.broadcasted_iota(jnp.int32, (rb, 1), 0)
    hpg = SSM_HEADS // SSM_GROUPS
    gw = hpg * SSM_HEAD_DIM

    ycols = []
    for g in range(SSM_GROUPS):
        bg = bmb[:, g * SSM_STATE:(g + 1) * SSM_STATE]
        cg = cmb[:, g * SSM_STATE:(g + 1) * SSM_STATE]
        xg = xd[:, g * gw:(g + 1) * gw]
        yg = jnp.zeros((rb, gw), F32)
        for r in range(rb):
            outer = _dot_tn(jnp.where(rows == r, xg, jnp.zeros_like(xg)), bg)
            news = []
            for e in range(hpg):
                hd = g * hpg + e
                dec = jnp.exp(a_col[r:r + 1, hd * LANES:(hd + 1) * LANES])
                new = st_ref[r, hd] * dec + outer[e * SSM_HEAD_DIM:(e + 1) * SSM_HEAD_DIM, :]
                so_ref[r, hd] = new
                news.append(new)
            yr = _dot_nt(cg, jnp.concatenate(news, axis=0).astype(BF16))
            yg = jnp.where(rows == r, yr, yg)
        ycols.append(yg)
    y = jnp.concatenate(ycols, axis=1) + xs * dskip_ref[...]
    y = y * _silu(z_ref[...])
    y_ref[...] = _group_rms(y, seg_ref[...], ng_ref[...], d_inner // SSM_GROUPS).astype(y_ref.dtype)


def _ssd_decode(zx, conv_state, ssm_state, conv_w, conv_b, dt_bias, a_log, d_skip, norm_g):
    r = zx.shape[0]
    d_inner = SSM_HEADS * SSM_HEAD_DIM
    rb = 8
    conv_dim = conv_w.shape[1]
    _, e64, e128, seg = _ssd_consts(d_inner)
    dt_blk = (2 * d_inner + 2 * SSM_GROUPS * SSM_STATE) // LANES
    cs = jnp.transpose(conv_state, (1, 0, 2))
    const = lambda i: (0, 0)
    st_spec = pl.BlockSpec((rb, SSM_HEADS, SSM_HEAD_DIM, SSM_STATE), lambda i: (i, 0, 0, 0))
    y, st = pl.pallas_call(
        functools.partial(_ssd_decode_kernel, rb=rb),
        grid=(r // rb,),
        in_specs=[
            pl.BlockSpec((rb, d_inner), lambda i: (i, 0)),
            pl.BlockSpec((rb, d_inner), lambda i: (i, 1)),
            pl.BlockSpec((rb, d_inner), lambda i: (i, 2)),
            pl.BlockSpec((rb, LANES), lambda i: (i, dt_blk)),
            pl.BlockSpec((SSM_CONV - 1, rb, conv_dim), lambda i: (0, i, 0)),
            st_spec,
            pl.BlockSpec((SSM_CONV, conv_dim), const),
            pl.BlockSpec((1, conv_dim), const),
            pl.BlockSpec((1, LANES), const),
            pl.BlockSpec((1, LANES), const),
            pl.BlockSpec((1, d_inner), const),
            pl.BlockSpec((1, d_inner), const),
            pl.BlockSpec(e64.shape, const),
            pl.BlockSpec(e128.shape, const),
            pl.BlockSpec(seg.shape, const),
        ],
        out_specs=[pl.BlockSpec((rb, d_inner), lambda i: (i, 0)), st_spec],
        out_shape=[jax.ShapeDtypeStruct((r, d_inner), BF16),
                   jax.ShapeDtypeStruct(ssm_state.shape, F32)],
        compiler_params=_params("parallel"),
        name="ssd_decode",
    )(zx, zx, zx, zx, cs, ssm_state, conv_w, conv_b.reshape(1, -1), _pad_lanes(dt_bias), _pad_lanes(a_log),
      jnp.repeat(d_skip, SSM_HEAD_DIM).reshape(1, d_inner), norm_g.reshape(1, d_inner), e64, e128, seg)
    return y, st


def _token_major(xt, heads):
    l, b, _, t = xt.shape
    n = len(heads)
    return jnp.transpose(xt.reshape((l, b) + heads + (t,)), (0, 1, n + 2) + tuple(range(2, n + 2)))


def kernel(x_prompt, x_sample, cache_sb_k, cache_sb_v, cache_diff_k, cache_diff_v, state_ssm_conv, state_ssm,
           page_table, ffn_norm, ffn_w_gate, ffn_w_up, ffn_w_down, mix_norm, sb_w_qkv, sb_w_o,
           diff_w_qkv, diff_q_norm, diff_k_norm, diff_lambda_q1, diff_lambda_k1, diff_lambda_q2, diff_lambda_k2,
           diff_subln, diff_w_o, ssm_w_in, ssm_conv_w, ssm_conv_b, ssm_dt_bias, ssm_a_log, ssm_d, ssm_norm,
           ssm_w_out):
    bp, seq, d = x_prompt.shape
    bs = x_sample.shape[0]
    depth = ffn_norm.shape[0]
    xp = x_prompt.reshape(bp * seq, d)
    xs = x_sample.reshape(bs, d)
    wg, wu, wd = ffn_w_gate.astype(BF16), ffn_w_up.astype(BF16), ffn_w_down.astype(BF16)

    sb_tp, sb_ks, sb_vs = [], [], []
    d_kp, d_vp, d_ks, d_vs = [], [], [], []
    cv_p, ss_p, cv_s, ss_s = [], [], [], []
    for i in range(depth):
        kind, j = i % N_MIXERS, i // N_MIXERS
        xp = _ffn(xp, ffn_norm[i, 0], wg, wu, wd, i, 0)
        xs = _ffn(xs, ffn_norm[i, 0], wg, wu, wd, i, 0)
        if kind == 0:
            w = sb_w_qkv[j].astype(BF16)
            wo = sb_w_o[j].astype(BF16)
            heads = (SB_HEADS, SB_HEAD_DIM)
            q, k, v, *sb_tp = _qkv(xp, mix_norm[i], w, bp, 2, layer=j, n_layers=cache_sb_k.shape[0],
                                   stacks=sb_tp)
            xp = _oproj(_sb_prompt(q, k, v, bp, seq), wo, xp)
            q, k, v, kt, vt = _qkv(xs, mix_norm[i], w, 1, 2)
            sb_ks.append(_token_major(kt, heads)[0, 0][:, None])
            sb_vs.append(_token_major(vt, heads)[0, 0][:, None])
            xs = _oproj(_sb_decode(q, cache_sb_k, cache_sb_v, j, page_table), wo, xs)
        elif kind == 1:
            lam_init = 0.8 - 0.6 * math.exp(-0.3 * i)
            w = diff_w_qkv[j].astype(BF16)
            wo = diff_w_o[j].astype(BF16)
            qg = jnp.tile(diff_q_norm[j], d // DIFF_HEAD_DIM)
            kg = jnp.tile(diff_k_norm[j], d // DIFF_HEAD_DIM)
            lam_params = jnp.stack([diff_lambda_q1[j], diff_lambda_k1[j], diff_lambda_q2[j], diff_lambda_k2[j]])
            heads = (2, DIFF_HEADS, DIFF_HEAD_DIM)
            q, k, v, kt = _qkv(xp, mix_norm[i], w, bp, 1, qg, kg)
            d_kp.append(_token_major(kt, heads)[0])
            d_vp.append(v.reshape(bp, seq, DIFF_HEADS, 2 * DIFF_HEAD_DIM))
            o = _diff_prompt(q, k, v, lam_params, diff_subln[j], lam_init, bp, seq)
            xp = _oproj(o, wo, xp)
            q, k, v, kt = _qkv(xs, mix_norm[i], w, 1, 1, qg, kg)
            d_ks.append(_token_major(kt, heads)[0, 0][:, None])
            d_vs.append(v.reshape(bs, 1, DIFF_HEADS, 2 * DIFF_HEAD_DIM))
            o = _diff_decode(q, k, v, cache_diff_k, cache_diff_v, j, page_table, lam_params, diff_subln[j],
                             lam_init)
            xs = _oproj(o, wo, xs)
        else:
            d_inner = SSM_HEADS * SSM_HEAD_DIM
            conv_dim = ssm_conv_w.shape[2]
            w_in = jnp.pad(ssm_w_in[j], ((0, 0), (0, LANES - SSM_HEADS))).astype(BF16)
            wo = ssm_w_out[j].astype(BF16)
            sp = (ssm_conv_w[j], ssm_conv_b[j], ssm_dt_bias[j], ssm_a_log[j], ssm_d[j], ssm_norm[j])
            tn = _tile(w_in.shape[1], 896)
            zx = _norm_matmul(xp, mix_norm[i], w_in, tn)
            y, st = _ssd_prompt(zx, *sp, bp, seq)
            cv_p.append(zx.reshape(bp, seq, -1)[:, seq - (SSM_CONV - 1):, d_inner:d_inner + conv_dim])
            ss_p.append(st)
            xp = _oproj(y, wo, xp)
            zx = _norm_matmul(xs, mix_norm[i], w_in, tn)
            y, st = _ssd_decode(zx, state_ssm_conv[j], state_ssm[j], *sp)
            cv_s.append(jnp.concatenate([state_ssm_conv[j][:, 1:], zx[:, None, d_inner:d_inner + conv_dim]], axis=1))
            ss_s.append(st)
            xs = _oproj(y, wo, xs)
        xp = _ffn(xp, ffn_norm[i, 1], wg, wu, wd, i, 1)
        xs = _ffn(xs, ffn_norm[i, 1], wg, wu, wd, i, 1)
    return (xp.reshape(bp, seq, d), xs.reshape(bs, 1, d),
            _token_major(sb_tp[0], (SB_HEADS, SB_HEAD_DIM)), _token_major(sb_tp[1], (SB_HEADS, SB_HEAD_DIM)),
            jnp.stack(sb_ks), jnp.stack(sb_vs),
            jnp.stack(d_kp), jnp.stack(d_vp), jnp.stack(d_ks), jnp.stack(d_vs),
            jnp.stack(cv_p), jnp.stack(ss_p), jnp.stack(cv_s), jnp.stack(ss_s))
```

```python
import functools
import math

import jax
import jax.numpy as jnp
from jax import lax
from jax.experimental import pallas as pl
from jax.experimental.pallas import tpu as pltpu

F32 = jnp.float32
BF16 = jnp.bfloat16

NORM_EPS = 1e-6
N_MIXERS = 3
SB_HEADS = 16
SB_HEAD_DIM = 64
DIFF_HEADS = 8
DIFF_HEAD_DIM = 64
SSM_HEAD_DIM = 64
SSM_HEADS = 32
SSM_GROUPS = 8
SSM_STATE = 128
SSM_CONV = 4
SSM_CHUNK = 128
LANES = 128
VMEM_LIMIT = 48 * 1024 * 1024


def _params(*sem):
    return pltpu.CompilerParams(dimension_semantics=sem, vmem_limit_bytes=VMEM_LIMIT)


def _tile(n, pref):
    t = min(n, pref)
    while n % t:
        t //= 2
    return t


def _dot(a, b):
    return jnp.dot(a, b, preferred_element_type=F32)


def _dot_nt(a, b):
    return lax.dot_general(a, b, (((1,), (1,)), ((), ())), preferred_element_type=F32)


def _dot_tn(a, b):
    return lax.dot_general(a, b, (((0,), (0,)), ((), ())), preferred_element_type=F32)


def _split_dot(x, m):
    hi = x.astype(BF16)
    lo = (x - hi.astype(F32)).astype(BF16)
    return _dot(hi, m) + _dot(lo, m)


def _split_dot_left(m, x):
    hi = x.astype(BF16)
    lo = (x - hi.astype(F32)).astype(BF16)
    return _dot(m, hi) + _dot(m, lo)


def _rms_rows(x, g):
    return x * lax.rsqrt(jnp.mean(x * x, axis=-1, keepdims=True) + NORM_EPS) * g


def _silu(x):
    return x / (1.0 + jnp.exp(-x))


def _softplus(z):
    return jnp.maximum(z, 0.0) + jnp.log(1.0 + jnp.exp(-jnp.abs(z)))


FFN_CHUNK = 512


def _ffn_kernel(x_ref, g_ref, wg_ref, wu_ref, wd_ref, o_ref, acc_scr):
    x = x_ref[...]
    h = _rms_rows(x, g_ref[...]).astype(BF16)
    dff = wg_ref.shape[1]
    for n, lo in enumerate(range(0, dff, FFN_CHUNK)):
        hi = min(lo + FFN_CHUNK, dff)
        gate = _dot(h, wg_ref[:, lo:hi])
        up = _dot(h, wu_ref[:, lo:hi])
        down = _dot((_silu(gate) * up).astype(BF16), wd_ref[lo:hi, :])
        if n == 0:
            acc_scr[...] = down
        else:
            acc_scr[...] += down
    o_ref[...] = x + 0.5 * acc_scr[...]


def _ffn(x, g, wg, wu, wd, layer, side):
    m, d = x.shape
    dff = wg.shape[3]
    tm = _tile(m, 1024)
    resident = dict(pipeline_mode=pl.Buffered(1))
    pick = lambda i: (layer, side, 0, 0)
    return pl.pallas_call(
        _ffn_kernel,
        grid=(m // tm,),
        in_specs=[
            pl.BlockSpec((tm, d), lambda i: (i, 0)),
            pl.BlockSpec((1, d), lambda i: (0, 0)),
            pl.BlockSpec((None, None, d, dff), pick, **resident),
            pl.BlockSpec((None, None, d, dff), pick, **resident),
            pl.BlockSpec((None, None, dff, d), pick, **resident),
        ],
        out_specs=pl.BlockSpec((tm, d), lambda i: (i, 0)),
        out_shape=jax.ShapeDtypeStruct((m, d), F32),
        scratch_shapes=[pltpu.VMEM((tm, d), F32)],
        compiler_params=_params("parallel"),
        name="ffn",
    )(x, g.reshape(1, d), wg, wu, wd)


def _norm_matmul_kernel(x_ref, g_ref, w_ref, o_ref, h_scr):
    @pl.when(pl.program_id(1) == 0)
    def _():
        h_scr[...] = _rms_rows(x_ref[...], g_ref[...]).astype(BF16)

    o_ref[...] = _dot(h_scr[...], w_ref[...])


def _norm_matmul(x, g, w, tn):
    m, d = x.shape
    n = w.shape[1]
    tm = _tile(m, 1024)
    return pl.pallas_call(
        _norm_matmul_kernel,
        grid=(m // tm, n // tn),
        in_specs=[
            pl.BlockSpec((tm, d), lambda i, j: (i, 0)),
            pl.BlockSpec((1, d), lambda i, j: (0, 0)),
            pl.BlockSpec((d, tn), lambda i, j: (0, j)),
        ],
        out_specs=pl.BlockSpec((tm, tn), lambda i, j: (i, j)),
        out_shape=jax.ShapeDtypeStruct((m, n), F32),
        scratch_shapes=[pltpu.VMEM((tm, d), BF16)],
        compiler_params=_params("parallel", "arbitrary"),
        name="norm_matmul",
    )(x, g.reshape(1, d), w)


def _head_rms(x, seg, g):
    cols = []
    for c in range(x.shape[1] // LANES):
        xc = x[:, c * LANES:(c + 1) * LANES]
        ms = _split_dot(xc * xc, seg)
        cols.append(xc * lax.rsqrt(ms + NORM_EPS))
    return jnp.concatenate(cols, axis=1) * g


def _qkv_kernel(x_ref, g_ref, wq_ref, wk_ref, wv_ref, seg_ref, qg_ref, kg_ref, *refs, qk_norm, n_prev):
    q_ref, k_ref, v_ref = refs[n_prev:n_prev + 3]
    t_refs = refs[n_prev + 3:]
    h = _rms_rows(x_ref[...], g_ref[...]).astype(BF16)
    q = _dot(h, wq_ref[...])
    k = _dot(h, wk_ref[...])
    if qk_norm:
        q = _head_rms(q, seg_ref[...], qg_ref[...])
        k = _head_rms(k, seg_ref[...], kg_ref[...])
    v = _dot(h, wv_ref[...])
    q_ref[...] = q
    k_ref[...] = k
    v_ref[...] = v
    for t_ref, val in zip(t_refs, (k, v)):
        t_ref[...] = val.T


def _seg_mean_matrix(n, group):
    r = jnp.arange(n)
    return jnp.where((r[:, None] // group) == (r[None, :] // group), 1.0 / group, 0.0).astype(BF16)


def _qkv(x, g, w, batch, n_transposed, qg=None, kg=None, layer=0, n_layers=1, stacks=()):
    m, d = x.shape
    seq = m // batch
    tm = _tile(seq, 512)
    nt = seq // tm
    qk_norm = qg is not None
    if not qk_norm:
        qg = kg = jnp.ones((d,), F32)
    row = lambda i: (i, 0)
    const = lambda i: (0, 0)
    out = jax.ShapeDtypeStruct((m, d), F32)
    t_spec = pl.BlockSpec((None, None, d, tm), lambda i: (layer, i // nt, 0, i % nt))
    t_out = jax.ShapeDtypeStruct((n_layers, batch, d, seq), F32)
    n_in = 8
    return pl.pallas_call(
        functools.partial(_qkv_kernel, qk_norm=qk_norm, n_prev=len(stacks)),
        grid=(m // tm,),
        in_specs=[
            pl.BlockSpec((tm, d), row),
            pl.BlockSpec((1, d), const),
            pl.BlockSpec((d, d), lambda i: (0, 0)),
            pl.BlockSpec((d, d), lambda i: (0, 1)),
            pl.BlockSpec((d, d), lambda i: (0, 2)),
            pl.BlockSpec((LANES, LANES), const),
            pl.BlockSpec((1, d), const),
            pl.BlockSpec((1, d), const),
        ] + [pl.BlockSpec(memory_space=pl.ANY)] * len(stacks),
        out_specs=[pl.BlockSpec((tm, d), row)] * 3 + [t_spec] * n_transposed,
        out_shape=[out, out, out] + [t_out] * n_transposed,
        input_output_aliases={n_in + p: 3 + p for p in range(len(stacks))},
        compiler_params=_params("parallel"),
        name="qkv_norm" if qk_norm else "qkv",
    )(x, g.reshape(1, d), w, w, w, _seg_mean_matrix(LANES, DIFF_HEAD_DIM),
      qg.reshape(1, d), kg.reshape(1, d), *stacks)


def _oproj_kernel(o_ref, w_ref, r_ref, y_ref):
    y_ref[...] = r_ref[...] + _dot(o_ref[...].astype(BF16), w_ref[...])


def _oproj(o, w, res):
    m, k = o.shape
    d = w.shape[1]
    tm = _tile(m, 512)
    return pl.pallas_call(
        _oproj_kernel,
        grid=(m // tm,),
        in_specs=[
            pl.BlockSpec((tm, k), lambda i: (i, 0)),
            pl.BlockSpec((k, d), lambda i: (0, 0)),
            pl.BlockSpec((tm, d), lambda i: (i, 0)),
        ],
        out_specs=pl.BlockSpec((tm, d), lambda i: (i, 0)),
        out_shape=jax.ShapeDtypeStruct((m, d), F32),
        compiler_params=_params("parallel"),
        name="oproj",
    )(o, w, res)


def _suffix_matrix(n):
    r = jnp.arange(n)
    return (r[:, None] > r[None, :]).astype(BF16)


def _sb_prompt_kernel(q_ref, k_ref, v_ref, u_ref, o_ref, kb_scr, vb_scr, acc_scr, *, tk, halves):
    qi = pl.program_id(2)

    @pl.when(qi == 0)
    def _():
        kb_scr[...] = k_ref[...].astype(BF16)
        vb_scr[...] = v_ref[...].astype(BF16)

    q2 = q_ref[...] * (SB_HEAD_DIM ** -0.5)
    lane = lax.broadcasted_iota(jnp.int32, (1, LANES), 1)
    row = lax.broadcasted_iota(jnp.int32, (tk, tk), 0)
    col = lax.broadcasted_iota(jnp.int32, (tk, tk), 1)
    causal = col < row
    u = u_ref[...]
    base = qi * halves
    chains = [(hh, a) for hh in range(2) for a in range(halves)]
    qms = []
    for hh, a in chains:
        in_head = (lane // SB_HEAD_DIM) == hh
        qms.append(jnp.where(in_head, q2[a * tk:(a + 1) * tk], 0.0).astype(BF16))

    def tiles(cs, kjs, rs, masked):
        starts = [pl.multiple_of(kj * tk, tk) for kj in kjs]
        zs = [_dot_nt(qms[c], kb_scr[pl.ds(st, tk), :]) for c, st in zip(cs, starts)]
        sps = [_softplus(z) for z in zs]
        if masked:
            sps = [jnp.where(causal, sp, 0.0) for sp in sps]
        cums = [_dot(sp.astype(BF16), u) for sp in sps]
        ws = [jnp.exp(z - sp - cum - r) for z, sp, cum, r in zip(zs, sps, cums, rs)]
        if masked:
            ws = [jnp.where(causal, w, 0.0) for w in ws]
        for c, st, w in zip(cs, starts, ws):
            pv = _dot(w.astype(BF16), vb_scr[pl.ds(st, tk), :])
            if masked:
                acc_scr[c] = pv
            else:
                acc_scr[c] += pv
        return [r + cum[:, :1] + sp[:, :1] for r, cum, sp in zip(rs, cums, sps)]

    n_chain = len(chains)
    rs = [jnp.zeros((tk, 1), F32)] * n_chain
    for t in range(halves):
        cs = [c for c in range(n_chain) if chains[c][1] >= t]
        new = tiles(cs, [base + chains[c][1] - t for c in cs], [rs[c] for c in cs], t == 0)
        for c, r in zip(cs, new):
            rs[c] = r

    depth = 2 if halves % 2 == 0 else 1

    def body(j, rs):
        kjs = [base - 1 - depth * j - t for t in range(depth)]
        starts = [pl.multiple_of(kj * tk, tk) for kj in kjs]
        zs = [[_dot_nt(qms[c], kb_scr[pl.ds(st, tk), :]) for c in range(n_chain)] for st in starts]
        sps = [[_softplus(z) for z in zt] for zt in zs]
        cums = [[_dot(sp.astype(BF16), u) for sp in spt] for spt in sps]
        rs = list(rs)
        for c in range(n_chain):
            pv = None
            for t in range(depth):
                w = jnp.exp(zs[t][c] - sps[t][c] - cums[t][c] - rs[c]).astype(BF16)
                d = _dot(w, vb_scr[pl.ds(starts[t], tk), :])
                pv = d if pv is None else pv + d
                rs[c] = rs[c] + cums[t][c][:, :1] + sps[t][c][:, :1]
            acc_scr[c] += pv
        return tuple(rs)

    lax.fori_loop(0, base // depth, body, tuple(rs))
    for a in range(halves):
        o_ref[a * tk:(a + 1) * tk, :] = jnp.where(
            lane < SB_HEAD_DIM, acc_scr[a], acc_scr[halves + a]).astype(o_ref.dtype)


ATTN_TK = 256
ATTN_HALVES = 2


def _attn_tiles(seq):
    tk = _tile(seq, ATTN_TK)
    halves = ATTN_HALVES if seq % (ATTN_HALVES * tk) == 0 else 1
    return tk, halves


def _sb_prompt(q, k, v, batch, seq):
    m, d = q.shape
    tk, halves = _attn_tiles(seq)
    tq = tk * halves
    nq = seq // tq
    pairs = d // LANES
    k3 = k.reshape(batch, seq, d)
    v3 = v.reshape(batch, seq, d)
    return pl.pallas_call(
        functools.partial(_sb_prompt_kernel, tk=tk, halves=halves),
        grid=(batch, pairs, nq),
        in_specs=[
            pl.BlockSpec((tq, LANES), lambda b, p, i: (b * nq + i, p)),
            pl.BlockSpec((None, seq, LANES), lambda b, p, i: (b, 0, p)),
            pl.BlockSpec((None, seq, LANES), lambda b, p, i: (b, 0, p)),
            pl.BlockSpec((tk, tk), lambda b, p, i: (0, 0)),
        ],
        out_specs=pl.BlockSpec((tq, LANES), lambda b, p, i: (b * nq + i, p)),
        out_shape=jax.ShapeDtypeStruct((m, d), BF16),
        scratch_shapes=[pltpu.VMEM((seq, LANES), BF16), pltpu.VMEM((seq, LANES), BF16),
                        pltpu.VMEM((2 * halves, tk, LANES), F32)],
        compiler_params=_params("parallel", "parallel", "arbitrary"),
        name="sb_prompt",
    )(q, k3, v3, _suffix_matrix(tk))


DECODE_PAGES_PER_STEP = 16


def _pages_per_step(n_pages):
    pps = DECODE_PAGES_PER_STEP
    while n_pages % pps:
        pps //= 2
    return pps


def _slot_minor(cache):
    nd = cache.ndim
    t = jnp.transpose(cache, (0, 1) + tuple(range(3, nd)) + (2,))
    return t.reshape(cache.shape[0], cache.shape[1], -1, cache.shape[2])


def _head_rows(x_row, n_rows):
    width = x_row.shape[1] // n_rows
    r = lax.broadcasted_iota(jnp.int32, (n_rows, n_rows * width), 0)
    c = lax.broadcasted_iota(jnp.int32, (n_rows, n_rows * width), 1)
    own = (c // width) == r
    return jnp.where(own, x_row, 0.0), own


def _sb_decode_kernel(pt_ref, q_ref, *refs, pps):
    k_refs, v_refs = refs[:pps], refs[pps:2 * pps]
    u_ref, o_ref, q_scr, acc_scr, r_scr = refs[2 * pps:]
    p = pl.program_id(1)

    @pl.when(p == 0)
    def _():
        qrows, _ = _head_rows(q_ref[0] * (SB_HEAD_DIM ** -0.5), SB_HEADS)
        q_scr[...] = qrows.astype(BF16)
        acc_scr[...] = jnp.zeros_like(acc_scr)
        r_scr[...] = jnp.zeros_like(r_scr)

    q = q_scr[...]
    r = r_scr[...]
    acc = acc_scr[...]
    zts = [_dot(q, k_ref[...].astype(BF16)) for k_ref in k_refs]
    sps = [_softplus(zt) for zt in zts]
    cums = [_split_dot(sp, u_ref[...]) for sp in sps]
    for zt, sp, cum, v_ref in zip(zts, sps, cums, v_refs):
        wb = jnp.exp(zt - sp - cum - r).astype(BF16)
        acc = acc + _dot_nt(wb, v_ref[...].astype(BF16))
        r = r + cum[:, :1] + sp[:, :1]
    acc_scr[...] = acc
    r_scr[...] = r

    @pl.when(p == pl.num_programs(1) - 1)
    def _():
        _, own = _head_rows(jnp.zeros((1, acc.shape[1]), F32), SB_HEADS)
        o_ref[0] = jnp.sum(jnp.where(own, acc, 0.0), axis=0, keepdims=True)


def _sb_decode(q, cache_k, cache_v, layer, page_table):
    r, d = q.shape
    n_pages = page_table.shape[1]
    page = cache_k.shape[2]
    pps = _pages_per_step(n_pages)
    pt = page_table.reshape(-1)

    cache_k, cache_v = _slot_minor(cache_k), _slot_minor(cache_v)

    def kv_map(i):
        return lambda b, p, pt_ref: (layer, pt_ref[b * n_pages + n_pages - 1 - (p * pps + i)], 0, 0)

    kv_specs = [pl.BlockSpec((None, None, d, page), kv_map(i)) for i in range(pps)]
    row_map = lambda b, p, pt_ref: (b, 0, 0)
    out = pl.pallas_call(
        functools.partial(_sb_decode_kernel, pps=pps),
        grid_spec=pltpu.PrefetchScalarGridSpec(
            num_scalar_prefetch=1,
            grid=(r, n_pages // pps),
            in_specs=[pl.BlockSpec((1, 1, d), row_map)] + kv_specs + kv_specs
            + [pl.BlockSpec((page, page), lambda b, p, pt_ref: (0, 0))],
            out_specs=pl.BlockSpec((1, 1, d), row_map),
            scratch_shapes=[pltpu.VMEM((SB_HEADS, d), BF16), pltpu.VMEM((SB_HEADS, d), F32),
                            pltpu.VMEM((SB_HEADS, 1), F32)],
        ),
        out_shape=jax.ShapeDtypeStruct((r, 1, d), F32),
        compiler_params=_params("parallel", "arbitrary"),
        name="sb_decode",
    )(pt, q.reshape(r, 1, d), *([cache_k] * pps), *([cache_v] * pps), _suffix_matrix(page))
    return out.reshape(r, d)


def _lambda(lq1, lk1, lq2, lk2, lam_init):
    return (jnp.exp(jnp.sum(lq1 * lk1, axis=-1, keepdims=True))
            - jnp.exp(jnp.sum(lq2 * lk2, axis=-1, keepdims=True)) + lam_init)


def _diff_prompt_kernel(slope_ref, q0_ref, q1_ref, k0_ref, k1_ref, v_ref, lam_ref, sg_ref, o_ref,
                        k0_scr, k1_scr, vb_scr, acc_scr, *, tk, halves, lam_init):
    h = pl.program_id(1)
    qi = pl.program_id(2)

    @pl.when(qi == 0)
    def _():
        k0_scr[...] = k0_ref[...].astype(BF16)
        k1_scr[...] = k1_ref[...].astype(BF16)
        vb_scr[...] = v_ref[...].astype(BF16)

    slope = slope_ref[h]
    lane = lax.broadcasted_iota(jnp.int32, (1, LANES), 1)
    in_head = (lane // DIFF_HEAD_DIM) == (h % 2)
    row = lax.broadcasted_iota(jnp.int32, (tk, tk), 0)
    col = lax.broadcasted_iota(jnp.int32, (tk, tk), 1)
    causal = col <= row
    rel = lax.broadcasted_iota(jnp.int32, (1, tk), 1).astype(F32)
    base = qi * halves
    k_scrs = (k0_scr, k1_scr)
    chains = [(c, a) for c in range(2) for a in range(halves)]
    qms = []
    for c, a in chains:
        q = (q0_ref, q1_ref)[c][a * tk:(a + 1) * tk, :]
        qms.append(jnp.where(in_head, q * (DIFF_HEAD_DIM ** -0.5), 0.0).astype(BF16))

    def tiles(cs, kjs, states, diag):
        starts = [pl.multiple_of(kj * tk, tk) for kj in kjs]
        ss = [_dot_nt(qms[ci], k_scrs[chains[ci][0]][pl.ds(st, tk), :])
              + slope * (rel + ((kj - base) * tk).astype(F32)) for ci, st, kj in zip(cs, starts, kjs)]
        if diag:
            ss = [jnp.where(causal, s, -jnp.inf) for s in ss]
            ms = [jnp.max(s, axis=-1, keepdims=True) for s in ss]
        else:
            ms = [jnp.maximum(m, jnp.max(s, axis=-1, keepdims=True)) for (m, _), s in zip(states, ss)]
        ps = [jnp.exp(s - m) for s, m in zip(ss, ms)]
        out = []
        for i, (ci, st, p) in enumerate(zip(cs, starts, ps)):
            pv = _dot(p.astype(BF16), vb_scr[pl.ds(st, tk), :])
            psum = jnp.sum(p, axis=-1, keepdims=True)
            if diag:
                acc_scr[ci] = pv
                out.append((ms[i], psum))
            else:
                alpha = jnp.exp(states[i][0] - ms[i])
                acc_scr[ci] = alpha * acc_scr[ci] + pv
                out.append((ms[i], alpha * states[i][1] + psum))
        return out

    n_chain = len(chains)
    states = [None] * n_chain
    for t in range(halves):
        cs = [ci for ci in range(n_chain) if chains[ci][1] >= t]
        new = tiles(cs, [base + chains[ci][1] - t for ci in cs], [states[ci] for ci in cs], t == 0)
        for ci, st in zip(cs, new):
            states[ci] = st

    def body(j, states):
        return tuple(tiles(list(range(n_chain)), [j] * n_chain, list(states), False))

    states = lax.fori_loop(0, base, body, tuple(states))
    lam = _lambda(lam_ref[0:1, :], lam_ref[1:2, :], lam_ref[2:3, :], lam_ref[3:4, :], lam_init)
    for a in range(halves):
        o = acc_scr[a] / states[a][1] - lam * (acc_scr[halves + a] / states[halves + a][1])
        o = o * lax.rsqrt(jnp.mean(o * o, axis=-1, keepdims=True) + NORM_EPS) * sg_ref[...]
        o_ref[a * tk:(a + 1) * tk, :] = (o * (1.0 - lam_init)).astype(o_ref.dtype)


def _alibi_slopes(n_heads):
    return jnp.exp2(-8.0 * jnp.arange(1, n_heads + 1, dtype=F32) / n_heads)


def _diff_prompt(q, k, v, lam_params, subln_g, lam_init, batch, seq):
    m, d = q.shape
    tk, halves = _attn_tiles(seq)
    tq = tk * halves
    nq = seq // tq
    half = d // (2 * LANES)
    k3 = k.reshape(batch, seq, d)
    v3 = v.reshape(batch, seq, d)
    return pl.pallas_call(
        functools.partial(_diff_prompt_kernel, tk=tk, halves=halves, lam_init=lam_init),
        grid_spec=pltpu.PrefetchScalarGridSpec(
            num_scalar_prefetch=0,
            grid=(batch, DIFF_HEADS, nq),
            in_specs=[
                pl.BlockSpec(memory_space=pltpu.SMEM),
                pl.BlockSpec((tq, LANES), lambda b, h, i: (b * nq + i, h // 2)),
                pl.BlockSpec((tq, LANES), lambda b, h, i: (b * nq + i, half + h // 2)),
                pl.BlockSpec((None, seq, LANES), lambda b, h, i: (b, 0, h // 2)),
                pl.BlockSpec((None, seq, LANES), lambda b, h, i: (b, 0, half + h // 2)),
                pl.BlockSpec((None, seq, LANES), lambda b, h, i: (b, 0, h)),
                pl.BlockSpec((4, DIFF_HEAD_DIM), lambda b, h, i: (0, 0)),
                pl.BlockSpec((1, LANES), lambda b, h, i: (0, 0)),
            ],
            out_specs=pl.BlockSpec((tq, LANES), lambda b, h, i: (b * nq + i, h)),
            scratch_shapes=[pltpu.VMEM((seq, LANES), BF16)] * 3 + [pltpu.VMEM((2 * halves, tk, LANES), F32)],
        ),
        out_shape=jax.ShapeDtypeStruct((m, d), BF16),
        compiler_params=_params("parallel", "parallel", "arbitrary"),
        name="diff_prompt",
    )(_alibi_slopes(DIFF_HEADS), q, q, k3, k3, v3, lam_params, subln_g.reshape(1, LANES))


def _diff_decode_kernel(pt_ref, q_ref, kn_ref, vn_ref, *refs, pps, lam_init, past):
    k_refs, v_refs = refs[:pps], refs[pps:2 * pps]
    slope_ref, lam_ref, sg_ref, rep_ref, o_ref, q_scr, acc_scr, m_scr, l_scr = refs[2 * pps:]
    p = pl.program_id(1)
    nrow = 2 * DIFF_HEADS
    page = k_refs[0].shape[1]

    @pl.when(p == 0)
    def _():
        qb = (q_ref[0] * (DIFF_HEAD_DIM ** -0.5)).astype(BF16).astype(F32)
        qrows, _ = _head_rows(qb, nrow)
        q_scr[...] = qrows.astype(BF16)
        kn = kn_ref[0].astype(BF16).astype(F32)
        m_scr[...] = jnp.sum(qrows * kn, axis=-1, keepdims=True)
        l_scr[...] = jnp.ones_like(l_scr)
        vn = vn_ref[0].astype(BF16).astype(F32)
        acc_scr[...] = jnp.concatenate([vn, vn], axis=0)

    q = q_scr[...]
    big = (nrow, page * DIFF_HEADS)
    own_head = (lax.broadcasted_iota(jnp.int32, big, 1) % DIFF_HEADS
                == lax.broadcasted_iota(jnp.int32, big, 0) % DIFF_HEADS)
    m, l, acc = m_scr[...], l_scr[...], acc_scr[...]
    ss = []
    for i, k_ref in enumerate(k_refs):
        pos = (p * pps + i) * page + lax.broadcasted_iota(jnp.int32, (1, page), 1)
        ss.append(_dot(q, k_ref[...].astype(BF16)) - slope_ref[...] * (past - pos).astype(F32))
    m_new = m
    for s in ss:
        m_new = jnp.maximum(m_new, jnp.max(s, axis=-1, keepdims=True))
    alpha = jnp.exp(m - m_new)
    prs = [jnp.exp(s - m_new) for s in ss]
    l = alpha * l
    acc = alpha * acc
    for pr, v_ref in zip(prs, v_refs):
        l = l + jnp.sum(pr, axis=-1, keepdims=True)
        spread = _dot(pr.astype(BF16), rep_ref[...])
        pbig = jnp.where(own_head, spread, 0.0).astype(BF16)
        acc = acc + _dot(pbig, v_ref[...].astype(BF16))
    m = m_new
    m_scr[...] = m
    l_scr[...] = l
    acc_scr[...] = acc

    @pl.when(p == pl.num_programs(1) - 1)
    def _():
        a = acc / l
        lam = _lambda(lam_ref[0:1, :], lam_ref[1:2, :], lam_ref[2:3, :], lam_ref[3:4, :], lam_init)
        dd = a[:DIFF_HEADS] - lam * a[DIFF_HEADS:]
        dn = dd * lax.rsqrt(jnp.mean(dd * dd, axis=-1, keepdims=True) + NORM_EPS)
        o_ref[0] = dn * sg_ref[...] * (1.0 - lam_init)


def _diff_decode(q, k_new, v_new, cache_k, cache_v, layer, page_table, lam_params, subln_g, lam_init):
    r, d = q.shape
    n_pages = page_table.shape[1]
    page = cache_k.shape[2]
    pps = _pages_per_step(n_pages)
    nrow = 2 * DIFF_HEADS
    vdim = 2 * DIFF_HEAD_DIM
    pt = page_table.reshape(-1)

    def page_of(i):
        return lambda b, p, pt_ref: pt_ref[b * n_pages + p * pps + i]

    cache_k = _slot_minor(cache_k)
    cache_v = cache_v.reshape(cache_v.shape[:2] + (page * DIFF_HEADS, vdim))
    rep = _expand_matrix(page, page, DIFF_HEADS)
    k_specs = [pl.BlockSpec((None, None, d, page),
                            lambda b, p, pt_ref, f=page_of(i): (layer, f(b, p, pt_ref), 0, 0))
               for i in range(pps)]
    v_specs = [pl.BlockSpec((None, None, page * DIFF_HEADS, vdim),
                            lambda b, p, pt_ref, f=page_of(i): (layer, f(b, p, pt_ref), 0, 0))
               for i in range(pps)]
    row_map = lambda b, p, pt_ref: (b, 0, 0)
    const = lambda b, p, pt_ref: (0, 0)
    slopes = jnp.tile(_alibi_slopes(DIFF_HEADS), 2).reshape(nrow, 1)
    out = pl.pallas_call(
        functools.partial(_diff_decode_kernel, pps=pps, lam_init=lam_init, past=n_pages * page),
        grid_spec=pltpu.PrefetchScalarGridSpec(
            num_scalar_prefetch=1,
            grid=(r, n_pages // pps),
            in_specs=[
                pl.BlockSpec((1, 1, d), row_map),
                pl.BlockSpec((1, 1, d), row_map),
                pl.BlockSpec((1, DIFF_HEADS, vdim), row_map),
            ] + k_specs + v_specs + [
                pl.BlockSpec((nrow, 1), const),
                pl.BlockSpec((4, DIFF_HEAD_DIM), const),
                pl.BlockSpec((1, vdim), const),
                pl.BlockSpec(rep.shape, const),
            ],
            out_specs=pl.BlockSpec((1, DIFF_HEADS, vdim), row_map),
            scratch_shapes=[pltpu.VMEM((nrow, d), BF16), pltpu.VMEM((nrow, vdim), F32),
                            pltpu.VMEM((nrow, 1), F32), pltpu.VMEM((nrow, 1), F32)],
        ),
        out_shape=jax.ShapeDtypeStruct((r, DIFF_HEADS, vdim), F32),
        compiler_params=_params("parallel", "arbitrary"),
        name="diff_decode",
    )(pt, q.reshape(r, 1, d), k_new.reshape(r, 1, d), v_new.reshape(r, DIFF_HEADS, vdim),
      *([cache_k] * pps), *([cache_v] * pps), slopes, lam_params, subln_g.reshape(1, vdim), rep)
    return out.reshape(r, d)


def _expand_matrix(n_in, n_heads, width):
    r = jnp.arange(n_in)[:, None]
    c = jnp.arange(n_heads * width)[None, :]
    return (r == c // width).astype(BF16)


def _split3_dot(x, m):
    hi = x.astype(BF16)
    r1 = x - hi.astype(F32)
    mid = r1.astype(BF16)
    lo = (r1 - mid.astype(F32)).astype(BF16)
    return _dot(hi, m) + _dot(mid, m) + _dot(lo, m)


def _group_rms(y, seg, g, group):
    cols = []
    for c in range(y.shape[1] // group):
        yc = y[:, c * group:(c + 1) * group]
        ms = _split_dot(yc * yc, seg)
        cols.append(yc * lax.rsqrt(ms + NORM_EPS))
    return jnp.concatenate(cols, axis=1) * g


def _ssd_prompt_kernel(z_ref, x_ref, bc_ref, dt_ref, cw_ref, cb_ref, dtb_ref, alog_ref, dskip_ref, ng_ref,
                       tri_ref, e64_ref, e128_ref, seg_ref, y_ref, st_ref, carry_scr, yscr, *, chunk):
    c = pl.program_id(1)
    d_inner = x_ref.shape[1]
    gn = SSM_GROUPS * SSM_STATE

    @pl.when(c == 0)
    def _():
        carry_scr[...] = jnp.zeros_like(carry_scr)
        st_ref[...] = jnp.zeros_like(st_ref)

    def conv(raw, prev, w, b):
        ext = jnp.concatenate([prev, raw], axis=0)
        out = b + w[SSM_CONV - 1:SSM_CONV, :] * raw
        for k in range(1, SSM_CONV):
            shifted = pltpu.roll(ext, k, axis=0)[8:, :]
            out = out + w[SSM_CONV - 1 - k:SSM_CONV - k, :] * shifted
        return _silu(out)

    x_raw = x_ref[...]
    bc_raw = bc_ref[...]
    cw = cw_ref[...]
    cb = cb_ref[...]
    xs = conv(x_raw, carry_scr[:, :d_inner], cw[:, :d_inner], cb[:, :d_inner])
    bcm = conv(bc_raw, carry_scr[:, d_inner:], cw[:, d_inner:], cb[:, d_inner:])
    carry_scr[:, :d_inner] = x_raw[chunk - 8:, :]
    carry_scr[:, d_inner:] = bc_raw[chunk - 8:, :]
    bm = bcm[:, :gn].astype(BF16)
    cm = bcm[:, gn:].astype(BF16)

    dt = _softplus(dt_ref[...] + dtb_ref[...])
    a = dt * (-jnp.exp(alog_ref[...]))
    tri = tri_ref[...]
    acs = _split_dot_left(tri, a)
    acs_t = acs.T
    acs_col = _split_dot(acs, e128_ref[...])
    acs_x = _split_dot(acs, e64_ref[...])
    dt_x = _split_dot(dt, e64_ref[...])
    xd = xs * dt_x
    last_x = acs_x[chunk - 1:chunk, :]
    xdw = (xd * jnp.exp(last_x - acs_x)).astype(BF16)
    xdb = xd.astype(BF16)
    ea_x = jnp.exp(acs_x)

    lrow = lax.broadcasted_iota(jnp.int32, (chunk, chunk), 0)
    scol = lax.broadcasted_iota(jnp.int32, (chunk, chunk), 1)
    tril = scol <= lrow
    lane = lax.broadcasted_iota(jnp.int32, (1, LANES), 1)
    first = lane < SSM_HEAD_DIM
    heads_per_group = SSM_HEADS // SSM_GROUPS

    for pair in range(SSM_HEADS // 2):
        g = (2 * pair) // heads_per_group
        bg = bm[:, g * SSM_STATE:(g + 1) * SSM_STATE]
        cg = cm[:, g * SSM_STATE:(g + 1) * SSM_STATE]
        cbm = _dot_nt(cg, bg)
        xd_pair = xdb[:, pair * LANES:(pair + 1) * LANES]
        ys = []
        for e in range(2):
            hd = 2 * pair + e
            seg = acs_col[:, hd * LANES:(hd + 1) * LANES] - acs_t[hd:hd + 1, :]
            decay = jnp.exp(jnp.where(tril, seg, -jnp.inf))
            ys.append(_dot((cbm * decay).astype(BF16), xd_pair))
        y_diag = jnp.where(first, ys[0], ys[1])
        st_pair = st_ref[2 * pair:2 * pair + 2].reshape(2 * SSM_HEAD_DIM, SSM_STATE)
        y_off = _dot_nt(cg, st_pair.astype(BF16)) * ea_x[:, pair * LANES:(pair + 1) * LANES]
        yscr[:, pair * LANES:(pair + 1) * LANES] = y_diag + y_off
        new = _dot_tn(xdw[:, pair * LANES:(pair + 1) * LANES], bg)
        for e in range(2):
            hd = 2 * pair + e
            dec = jnp.exp(acs_col[chunk - 1:chunk, hd * LANES:(hd + 1) * LANES])
            st_ref[hd] = st_ref[hd] * dec + new[e * SSM_HEAD_DIM:(e + 1) * SSM_HEAD_DIM, :]

    y = yscr[...] + xs * dskip_ref[...]
    y = y * _silu(z_ref[...])
    y_ref[...] = _group_rms(y, seg_ref[...], ng_ref[...], d_inner // SSM_GROUPS).astype(y_ref.dtype)


def _ssd_consts(d_inner):
    r = jnp.arange(SSM_CHUNK)
    tri = (r[:, None] >= r[None, :]).astype(BF16)
    return (tri, _expand_matrix(LANES, SSM_HEADS, SSM_HEAD_DIM), _expand_matrix(LANES, SSM_HEADS, LANES),
            _seg_mean_matrix(d_inner // SSM_GROUPS, d_inner // SSM_GROUPS))


def _pad_lanes(v):
    return jnp.pad(v, (0, LANES - v.shape[0])).reshape(1, LANES)


def _ssd_prompt(zx, conv_w, conv_b, dt_bias, a_log, d_skip, norm_g, batch, seq):
    d_inner = SSM_HEADS * SSM_HEAD_DIM
    chunk = SSM_CHUNK
    nc = seq // chunk
    conv_dim = conv_w.shape[1]
    tri, e64, e128, seg = _ssd_consts(d_inner)
    dt_blk = (2 * d_inner + 2 * SSM_GROUPS * SSM_STATE) // LANES
    const = lambda b, c: (0, 0)
    y, st = pl.pallas_call(
        functools.partial(_ssd_prompt_kernel, chunk=chunk),
        grid=(batch, nc),
        in_specs=[
            pl.BlockSpec((chunk, d_inner), lambda b, c: (b * nc + c, 0)),
            pl.BlockSpec((chunk, d_inner), lambda b, c: (b * nc + c, 1)),
            pl.BlockSpec((chunk, d_inner), lambda b, c: (b * nc + c, 2)),
            pl.BlockSpec((chunk, LANES), lambda b, c: (b * nc + c, dt_blk)),
            pl.BlockSpec((SSM_CONV, conv_dim), const),
            pl.BlockSpec((1, conv_dim), const),
            pl.BlockSpec((1, LANES), const),
            pl.BlockSpec((1, LANES), const),
            pl.BlockSpec((1, d_inner), const),
            pl.BlockSpec((1, d_inner), const),
            pl.BlockSpec(tri.shape, const),
            pl.BlockSpec(e64.shape, const),
            pl.BlockSpec(e128.shape, const),
            pl.BlockSpec(seg.shape, const),
        ],
        out_specs=[
            pl.BlockSpec((chunk, d_inner), lambda b, c: (b * nc + c, 0)),
            pl.BlockSpec((None, SSM_HEADS, SSM_HEAD_DIM, SSM_STATE), lambda b, c: (b, 0, 0, 0)),
        ],
        out_shape=[jax.ShapeDtypeStruct((batch * seq, d_inner), BF16),
                   jax.ShapeDtypeStruct((batch, SSM_HEADS, SSM_HEAD_DIM, SSM_STATE), F32)],
        scratch_shapes=[pltpu.VMEM((8, conv_dim), F32), pltpu.VMEM((chunk, d_inner), F32)],
        compiler_params=_params("parallel", "arbitrary"),
        name="ssd_prompt",
    )(zx, zx, zx, zx, conv_w, conv_b.reshape(1, -1), _pad_lanes(dt_bias), _pad_lanes(a_log),
      jnp.repeat(d_skip, SSM_HEAD_DIM).reshape(1, d_inner), norm_g.reshape(1, d_inner), tri, e64, e128, seg)
    return y, st


def _ssd_decode_kernel(z_ref, x_ref, bc_ref, dt_ref, cs_ref, st_ref, cw_ref, cb_ref, dtb_ref, alog_ref,
                       dskip_ref, ng_ref, e64_ref, e128_ref, seg_ref, y_ref, so_ref, *, rb):
    d_inner = x_ref.shape[1]
    gn = SSM_GROUPS * SSM_STATE
    cw = cw_ref[...]
    xbc_raw = jnp.concatenate([x_ref[...], bc_ref[...]], axis=1)
    out = cb_ref[...] + cw[SSM_CONV - 1:SSM_CONV, :] * xbc_raw
    for j in range(SSM_CONV - 1):
        out = out + cw[j:j + 1, :] * cs_ref[j]
    xbc = _silu(out)
    xs = xbc[:, :d_inner]
    bm = xbc[:, d_inner:d_inner + gn]
    cm = xbc[:, d_inner + gn:]

    dt = _softplus(dt_ref[...] + dtb_ref[...])
    a = dt * (-jnp.exp(alog_ref[...]))
    a_col = _split3_dot(a, e128_ref[...])
    dt_x = _split3_dot(dt, e64_ref[...])
    xd = (xs * dt_x).astype(BF16)
    bmb = bm.astype(BF16)
    cmb = cm.astype(BF16)
    rows = lax.broadcasted_iota(jnp.int32, (rb, 1), 0)
    hpg = SSM_HEADS // SSM_GROUPS
    gw = hpg * SSM_HEAD_DIM

    ycols = []
    for g in range(SSM_GROUPS):
        bg = bmb[:, g * SSM_STATE:(g + 1) * SSM_STATE]
        cg = cmb[:, g * SSM_STATE:(g + 1) * SSM_STATE]
        xg = xd[:, g * gw:(g + 1) * gw]
        yg = jnp.zeros((rb, gw), F32)
        for r in range(rb):
            outer = _dot_tn(jnp.where(rows == r, xg, jnp.zeros_like(xg)), bg)
            news = []
            for e in range(hpg):
                hd = g * hpg + e
                dec = jnp.exp(a_col[r:r + 1, hd * LANES:(hd + 1) * LANES])
                new = st_ref[r, hd] * dec + outer[e * SSM_HEAD_DIM:(e + 1) * SSM_HEAD_DIM, :]
                so_ref[r, hd] = new
                news.append(new)
            yr = _dot_nt(cg, jnp.concatenate(news, axis=0).astype(BF16))
            yg = jnp.where(rows == r, yr, yg)
        ycols.append(yg)
    y = jnp.concatenate(ycols, axis=1) + xs * dskip_ref[...]
    y = y * _silu(z_ref[...])
    y_ref[...] = _group_rms(y, seg_ref[...], ng_ref[...], d_inner // SSM_GROUPS).astype(y_ref.dtype)


def _ssd_decode(zx, conv_state, ssm_state, conv_w, conv_b, dt_bias, a_log, d_skip, norm_g):
    r = zx.shape[0]
    d_inner = SSM_HEADS * SSM_HEAD_DIM
    rb = 8
    conv_dim = conv_w.shape[1]
    _, e64, e128, seg = _ssd_consts(d_inner)
    dt_blk = (2 * d_inner + 2 * SSM_GROUPS * SSM_STATE) // LANES
    cs = jnp.transpose(conv_state, (1, 0, 2))
    const = lambda i: (0, 0)
    st_spec = pl.BlockSpec((rb, SSM_HEADS, SSM_HEAD_DIM, SSM_STATE), lambda i: (i, 0, 0, 0))
    y, st = pl.pallas_call(
        functools.partial(_ssd_decode_kernel, rb=rb),
        grid=(r // rb,),
        in_specs=[
            pl.BlockSpec((rb, d_inner), lambda i: (i, 0)),
            pl.BlockSpec((rb, d_inner), lambda i: (i, 1)),
            pl.BlockSpec((rb, d_inner), lambda i: (i, 2)),
            pl.BlockSpec((rb, LANES), lambda i: (i, dt_blk)),
            pl.BlockSpec((SSM_CONV - 1, rb, conv_dim), lambda i: (0, i, 0)),
            st_spec,
            pl.BlockSpec((SSM_CONV, conv_dim), const),
            pl.BlockSpec((1, conv_dim), const),
            pl.BlockSpec((1, LANES), const),
            pl.BlockSpec((1, LANES), const),
            pl.BlockSpec((1, d_inner), const),
            pl.BlockSpec((1, d_inner), const),
            pl.BlockSpec(e64.shape, const),
            pl.BlockSpec(e128.shape, const),
            pl.BlockSpec(seg.shape, const),
        ],
        out_specs=[pl.BlockSpec((rb, d_inner), lambda i: (i, 0)), st_spec],
        out_shape=[jax.ShapeDtypeStruct((r, d_inner), BF16),
                   jax.ShapeDtypeStruct(ssm_state.shape, F32)],
        compiler_params=_params("parallel"),
        name="ssd_decode",
    )(zx, zx, zx, zx, cs, ssm_state, conv_w, conv_b.reshape(1, -1), _pad_lanes(dt_bias), _pad_lanes(a_log),
      jnp.repeat(d_skip, SSM_HEAD_DIM).reshape(1, d_inner), norm_g.reshape(1, d_inner), e64, e128, seg)
    return y, st


def _token_major(xt, heads):
    l, b, _, t = xt.shape
    n = len(heads)
    return jnp.transpose(xt.reshape((l, b) + heads + (t,)), (0, 1, n + 2) + tuple(range(2, n + 2)))


def kernel(x_prompt, x_sample, cache_sb_k, cache_sb_v, cache_diff_k, cache_diff_v, state_ssm_conv, state_ssm,
           page_table, ffn_norm, ffn_w_gate, ffn_w_up, ffn_w_down, mix_norm, sb_w_qkv, sb_w_o,
           diff_w_qkv, diff_q_norm, diff_k_norm, diff_lambda_q1, diff_lambda_k1, diff_lambda_q2, diff_lambda_k2,
           diff_subln, diff_w_o, ssm_w_in, ssm_conv_w, ssm_conv_b, ssm_dt_bias, ssm_a_log, ssm_d, ssm_norm,
           ssm_w_out):
    bp, seq, d = x_prompt.shape
    bs = x_sample.shape[0]
    depth = ffn_norm.shape[0]
    xp = x_prompt.reshape(bp * seq, d)
    xs = x_sample.reshape(bs, d)
    wg, wu, wd = ffn_w_gate.astype(BF16), ffn_w_up.astype(BF16), ffn_w_down.astype(BF16)

    sb_tp, sb_ks, sb_vs = [], [], []
    d_kp, d_vp, d_ks, d_vs = [], [], [], []
    cv_p, ss_p, cv_s, ss_s = [], [], [], []
    for i in range(depth):
        kind, j = i % N_MIXERS, i // N_MIXERS
        xp = _ffn(xp, ffn_norm[i, 0], wg, wu, wd, i, 0)
        xs = _ffn(xs, ffn_norm[i, 0], wg, wu, wd, i, 0)
        if kind == 0:
            w = sb_w_qkv[j].astype(BF16)
            wo = sb_w_o[j].astype(BF16)
            heads = (SB_HEADS, SB_HEAD_DIM)
            q, k, v, *sb_tp = _qkv(xp, mix_norm[i], w, bp, 2, layer=j, n_layers=cache_sb_k.shape[0],
                                   stacks=sb_tp)
            xp = _oproj(_sb_prompt(q, k, v, bp, seq), wo, xp)
            q, k, v, kt, vt = _qkv(xs, mix_norm[i], w, 1, 2)
            sb_ks.append(_token_major(kt, heads)[0, 0][:, None])
            sb_vs.append(_token_major(vt, heads)[0, 0][:, None])
            xs = _oproj(_sb_decode(q, cache_sb_k, cache_sb_v, j, page_table), wo, xs)
        elif kind == 1:
            lam_init = 0.8 - 0.6 * math.exp(-0.3 * i)
            w = diff_w_qkv[j].astype(BF16)
            wo = diff_w_o[j].astype(BF16)
            qg = jnp.tile(diff_q_norm[j], d // DIFF_HEAD_DIM)
            kg = jnp.tile(diff_k_norm[j], d // DIFF_HEAD_DIM)
            lam_params = jnp.stack([diff_lambda_q1[j], diff_lambda_k1[j], diff_lambda_q2[j], diff_lambda_k2[j]])
            heads = (2, DIFF_HEADS, DIFF_HEAD_DIM)
            q, k, v, kt = _qkv(xp, mix_norm[i], w, bp, 1, qg, kg)
            d_kp.append(_token_major(kt, heads)[0])
            d_vp.append(v.reshape(bp, seq, DIFF_HEADS, 2 * DIFF_HEAD_DIM))
            o = _diff_prompt(q, k, v, lam_params, diff_subln[j], lam_init, bp, seq)
            xp = _oproj(o, wo, xp)
            q, k, v, kt = _qkv(xs, mix_norm[i], w, 1, 1, qg, kg)
            d_ks.append(_token_major(kt, heads)[0, 0][:, None])
            d_vs.append(v.reshape(bs, 1, DIFF_HEADS, 2 * DIFF_HEAD_DIM))
            o = _diff_decode(q, k, v, cache_diff_k, cache_diff_v, j, page_table, lam_params, diff_subln[j],
                             lam_init)
            xs = _oproj(o, wo, xs)
        else:
            d_inner = SSM_HEADS * SSM_HEAD_DIM
            conv_dim = ssm_conv_w.shape[2]
            w_in = jnp.pad(ssm_w_in[j], ((0, 0), (0, LANES - SSM_HEADS))).astype(BF16)
            wo = ssm_w_out[j].astype(BF16)
            sp = (ssm_conv_w[j], ssm_conv_b[j], ssm_dt_bias[j], ssm_a_log[j], ssm_d[j], ssm_norm[j])
            tn = _tile(w_in.shape[1], 896)
            zx = _norm_matmul(xp, mix_norm[i], w_in, tn)
            y, st = _ssd_prompt(zx, *sp, bp, seq)
            cv_p.append(zx.reshape(bp, seq, -1)[:, seq - (SSM_CONV - 1):, d_inner:d_inner + conv_dim])
            ss_p.append(st)
            xp = _oproj(y, wo, xp)
            zx = _norm_matmul(xs, mix_norm[i], w_in, tn)
            y, st = _ssd_decode(zx, state_ssm_conv[j], state_ssm[j], *sp)
            cv_s.append(jnp.concatenate([state_ssm_conv[j][:, 1:], zx[:, None, d_inner:d_inner + conv_dim]], axis=1))
            ss_s.append(st)
            xs = _oproj(y, wo, xs)
        xp = _ffn(xp, ffn_norm[i, 1], wg, wu, wd, i, 1)
        xs = _ffn(xs, ffn_norm[i, 1], wg, wu, wd, i, 1)
    return (xp.reshape(bp, seq, d), xs.reshape(bs, 1, d),
            _token_major(sb_tp[0], (SB_HEADS, SB_HEAD_DIM)), _token_major(sb_tp[1], (SB_HEADS, SB_HEAD_DIM)),
            jnp.stack(sb_ks), jnp.stack(sb_vs),
            jnp.stack(d_kp), jnp.stack(d_vp), jnp.stack(d_ks), jnp.stack(d_vs),
            jnp.stack(cv_p), jnp.stack(ss_p), jnp.stack(cv_s), jnp.stack(ss_s))
```
